```python
import math
import jax, jax.numpy as jnp
from jax import lax
import numpy as np

D_MODEL = 1024
BATCH = 8
SEQ = 4096
DEPTH = 4

N_HEADS = 8
HEAD_DIM = D_MODEL // (2 * N_HEADS)
V_DIM = 2 * HEAD_DIM
ATTN_WIDTH = N_HEADS * V_DIM
CONV_CH = D_MODEL
CONV_WIDTH = 31
FF_DIM = ((8 * D_MODEL // 3 + 255) // 256) * 256
NUM_BUCKETS = 32
MAX_DISTANCE = 128
Q_BLOCK = 128
EPS = 1e-6

Q_COLS = N_HEADS * 2 * HEAD_DIM
K_COLS = N_HEADS * 2 * HEAD_DIM
V_COLS = N_HEADS * V_DIM
U_COLS = 2 * CONV_CH
G_COLS = 2 * D_MODEL
IN_COLS = Q_COLS + K_COLS + V_COLS + U_COLS + G_COLS

kernel_name = "hybrid_diffattn_conformer_gated"


def rms_norm(x, w):
    xf = x.astype(jnp.float32)
    y = xf * lax.rsqrt(jnp.mean(xf * xf, axis=-1, keepdims=True) + EPS)
    return (y * w.astype(jnp.float32)).astype(x.dtype)


def layer_norm(x, w, b):
    xf = x.astype(jnp.float32)
    mu = jnp.mean(xf, axis=-1, keepdims=True)
    xc = xf - mu
    var = jnp.mean(xc * xc, axis=-1, keepdims=True)
    y = xc * lax.rsqrt(var + EPS) * w.astype(jnp.float32) + b.astype(jnp.float32)
    return y.astype(x.dtype)


def t5_bucket(rel):
    n = jnp.maximum(rel, 0)
    max_exact = NUM_BUCKETS // 2
    nf = jnp.maximum(n, 1).astype(jnp.float32)
    large = max_exact + (jnp.log(nf / max_exact) / math.log(MAX_DISTANCE / max_exact)
                         * (NUM_BUCKETS - max_exact)).astype(jnp.int32)
    large = jnp.minimum(large, NUM_BUCKETS - 1)
    return jnp.where(n < max_exact, n, large)


def diff_attention(q1, q2, k1, k2, v, lam, rel_bias):
    B, H, S, _ = q1.shape
    nb = S // Q_BLOCK
    scale = HEAD_DIM ** -0.5
    f32 = jnp.float32
    k1f, k2f, vf = k1.astype(f32), k2.astype(f32), v.astype(f32)
    lamf = lam.astype(f32)
    k_pos = jnp.arange(S, dtype=jnp.int32)
    bias_tab = rel_bias.astype(f32)

    def block(i):
        start = i * Q_BLOCK
        qb1 = lax.dynamic_slice_in_dim(q1, start, Q_BLOCK, axis=2).astype(f32)
        qb2 = lax.dynamic_slice_in_dim(q2, start, Q_BLOCK, axis=2).astype(f32)
        q_pos = start + jnp.arange(Q_BLOCK, dtype=jnp.int32)
        rel = q_pos[:, None] - k_pos[None, :]
        bias = jnp.transpose(bias_tab[t5_bucket(rel)], (2, 0, 1))
        causal = rel >= 0

        def probs(qb, kf):
            s = jnp.einsum('bhqd,bhkd->bhqk', qb, kf) * scale + bias
            s = jnp.where(causal, s, -jnp.inf)
            return jax.nn.softmax(s, axis=-1)

        p = probs(qb1, k1f) - lamf * probs(qb2, k2f)
        return jnp.einsum('bhqk,bhkd->bhqd', p, vf)

    out = lax.map(block, jnp.arange(nb))
    out = jnp.transpose(out, (1, 0, 3, 2, 4)).reshape(B, S, H, V_DIM)
    return out.astype(q1.dtype)


def causal_depthwise_conv(u, w, b):
    up = jnp.pad(u, ((0, 0), (CONV_WIDTH - 1, 0), (0, 0)))
    y = lax.conv_general_dilated(up, w[:, None, :], window_strides=(1,), padding='VALID',
                                 dimension_numbers=('NWC', 'WIO', 'NWC'),
                                 feature_group_count=CONV_CH)
    return y + b


def setup_inputs(seed: int = 0) -> dict:
    key = jax.random.key(seed)
    ks = jax.random.split(key, 24)
    n = jax.random.normal
    f = jnp.float32

    def gain(k, shape):
        return 1.0 + 0.05 * n(k, shape, f)

    return {
        "x": n(ks[0], (BATCH, SEQ, D_MODEL), f),
        "attn_norm_w": gain(ks[1], (DEPTH, D_MODEL)),
        "w_in": n(ks[2], (DEPTH, D_MODEL, IN_COLS), f) * D_MODEL ** -0.5,
        "q_norm_w": gain(ks[3], (DEPTH, HEAD_DIM)),
        "k_norm_w": gain(ks[4], (DEPTH, HEAD_DIM)),
        "lam_q1": 0.1 * n(ks[5], (DEPTH, HEAD_DIM), f),
        "lam_k1": 0.1 * n(ks[6], (DEPTH, HEAD_DIM), f),
        "lam_q2": 0.1 * n(ks[7], (DEPTH, HEAD_DIM), f),
        "lam_k2": 0.1 * n(ks[8], (DEPTH, HEAD_DIM), f),
        "subln_w": gain(ks[9], (DEPTH, V_DIM)),
        "w_proj_attn": n(ks[10], (DEPTH, ATTN_WIDTH, D_MODEL), f) * ATTN_WIDTH ** -0.5,
        "conv_w": n(ks[11], (DEPTH, CONV_WIDTH, CONV_CH), f) * CONV_WIDTH ** -0.5,
        "conv_b": 0.02 * n(ks[12], (DEPTH, CONV_CH), f),
        "conv_ln_w": gain(ks[13], (DEPTH, CONV_CH)),
        "conv_ln_b": 0.02 * n(ks[14], (DEPTH, CONV_CH), f),
        "w_proj_conv": n(ks[15], (DEPTH, CONV_CH, D_MODEL), f) * CONV_CH ** -0.5,
        "gate_b": 0.02 * n(ks[16], (DEPTH, G_COLS), f),
        "w_out": n(ks[17], (DEPTH, D_MODEL, D_MODEL), f) * D_MODEL ** -0.5,
        "ffn_norm_w": gain(ks[18], (DEPTH, D_MODEL)),
        "w_gate_up": n(ks[19], (DEPTH, D_MODEL, 2 * FF_DIM), f) * D_MODEL ** -0.5,
        "w_down": n(ks[20], (DEPTH, FF_DIM, D_MODEL), f) * FF_DIM ** -0.5,
        "rel_bias": 0.5 * n(ks[21], (NUM_BUCKETS, N_HEADS), f),
    }


def reference(x, attn_norm_w, w_in, q_norm_w, k_norm_w, lam_q1, lam_k1, lam_q2, lam_k2,
              subln_w, w_proj_attn, conv_w, conv_b, conv_ln_w, conv_ln_b, w_proj_conv,
              gate_b, w_out, ffn_norm_w, w_gate_up, w_down, rel_bias):
    B, S, _ = x.shape
    splits = np.cumsum([Q_COLS, K_COLS, V_COLS, U_COLS]).tolist()
    for l in range(DEPTH):
        lam_init = 0.8 - 0.6 * math.exp(-0.3 * l)
        h = rms_norm(x, attn_norm_w[l])
        proj = h @ w_in[l]
        q, k, v, u, g = jnp.split(proj, splits, axis=-1)

        q = rms_norm(q.reshape(B, S, N_HEADS, 2, HEAD_DIM), q_norm_w[l])
        k = rms_norm(k.reshape(B, S, N_HEADS, 2, HEAD_DIM), k_norm_w[l])
        q = jnp.transpose(q, (3, 0, 2, 1, 4))
        k = jnp.transpose(k, (3, 0, 2, 1, 4))
        v = jnp.transpose(v.reshape(B, S, N_HEADS, V_DIM), (0, 2, 1, 3))
        lam = (jnp.exp(jnp.sum(lam_q1[l].astype(jnp.float32) * lam_k1[l].astype(jnp.float32)))
               - jnp.exp(jnp.sum(lam_q2[l].astype(jnp.float32) * lam_k2[l].astype(jnp.float32)))
               + lam_init)
        a = diff_attention(q[0], q[1], k[0], k[1], v, lam, rel_bias)
        a = rms_norm(a, subln_w[l]) * (1.0 - lam_init)
        y_a = a.reshape(B, S, ATTN_WIDTH) @ w_proj_attn[l]

        u_a, u_b = jnp.split(u, 2, axis=-1)
        c = causal_depthwise_conv(u_a * jax.nn.sigmoid(u_b), conv_w[l], conv_b[l])
        c = jax.nn.silu(layer_norm(c, conv_ln_w[l], conv_ln_b[l]))
        y_b = c @ w_proj_conv[l]

        g_a, g_b = jnp.split(jax.nn.sigmoid(g + gate_b[l]), 2, axis=-1)
        x = x + (g_a * y_a + g_b * y_b) @ w_out[l]

        h = rms_norm(x, ffn_norm_w[l])
        gt, up = jnp.split(h @ w_gate_up[l], 2, axis=-1)
        x = x + (jax.nn.silu(gt) * up) @ w_down[l]
    return x
```

```python
import functools
import math

import numpy as np
import jax
import jax.numpy as jnp
from jax import lax
from jax.experimental import pallas as pl
from jax.experimental.pallas import tpu as pltpu

N_HEADS = 8
HEAD_DIM = 64
V_DIM = 2 * HEAD_DIM
CONV_WIDTH = 31
NUM_BUCKETS = 32
MAX_DISTANCE = 128
EPS = 1e-6
LOG2E = math.log2(math.e)

LANES = 128
ROW_TILE = 512
ATT_TILE = 256
CONV_TILE = 256
CONV_CHUNK = 32
CONV_HALO = 32
FF_CHUNK = 256
VMEM_LIMIT = 56 * 1024 * 1024

BF16 = jnp.bfloat16
F32 = jnp.float32


def _t5_bucket_starts():
    rel = np.arange(0, 4 * MAX_DISTANCE, dtype=np.int32)
    max_exact = NUM_BUCKETS // 2
    nf = np.maximum(rel, 1).astype(np.float32)
    large = max_exact + (np.log(nf / np.float32(max_exact)) / np.float32(math.log(MAX_DISTANCE / max_exact))
                         * np.float32(NUM_BUCKETS - max_exact)).astype(np.int32)
    large = np.minimum(large, NUM_BUCKETS - 1)
    bucket = np.where(rel < max_exact, rel, large)
    starts = [int(np.argmax(bucket == b)) for b in range(NUM_BUCKETS)]
    assert all(bucket[s] == b for b, s in enumerate(starts)) and starts == sorted(starts)
    return starts


_BUCKET_STARTS = _t5_bucket_starts()
assert _BUCKET_STARTS[-1] <= ATT_TILE // 2


def _resident(shape):
    return pl.BlockSpec(shape, lambda *_: (0,) * len(shape), pipeline_mode=pl.Buffered(1))


def _dot(a, b):
    return jnp.dot(a, b, preferred_element_type=F32)


def _sigmoid(x):
    return 1.0 / (1.0 + jnp.exp(-x))


def _rms_rows(x, w):
    ms = jnp.mean(x * x, axis=-1, keepdims=True)
    return x * lax.rsqrt(ms + EPS) * w


def _head_rms(acc, w128):
    rows = acc.shape[0]
    lo = lax.broadcasted_iota(jnp.int32, (rows, LANES), 1) < HEAD_DIM
    outs = []
    for hh in range(acc.shape[1] // LANES):
        blk = acc[:, hh * LANES:(hh + 1) * LANES]
        sq = blk * blk
        s_lo = jnp.sum(jnp.where(lo, sq, 0.0), axis=-1, keepdims=True)
        s_hi = jnp.sum(jnp.where(lo, 0.0, sq), axis=-1, keepdims=True)
        r = jnp.where(lo, lax.rsqrt(s_lo / HEAD_DIM + EPS), lax.rsqrt(s_hi / HEAD_DIM + EPS))
        outs.append(blk * r * w128)
    return jnp.concatenate(outs, axis=1)


def _inproj_kernel(x_ref, nw_ref, wqkv_ref, wua_ref, wub_ref, wg_ref, gb_ref, qw_ref, kw_ref,
                   qkv_ref, glu_ref, gate_ref, *, chunk):
    d_attn = N_HEADS * V_DIM
    h = _rms_rows(x_ref[...], nw_ref[...]).astype(BF16)
    qw = qw_ref[...] * (HEAD_DIM ** -0.5 * LOG2E)
    kw = kw_ref[...]
    for c in range(3 * d_attn // chunk):
        cols = slice(c * chunk, (c + 1) * chunk)
        acc = _dot(h, wqkv_ref[:, cols])
        if c * chunk < d_attn:
            acc = _head_rms(acc, qw)
        elif c * chunk < 2 * d_attn:
            acc = _head_rms(acc, kw)
        qkv_ref[:, cols] = acc.astype(BF16)
    for c in range(wua_ref.shape[1] // chunk):
        cols = slice(c * chunk, (c + 1) * chunk)
        ua = _dot(h, wua_ref[:, cols])
        ub = _dot(h, wub_ref[:, cols])
        glu_ref[:, cols] = (ua * _sigmoid(ub)).astype(BF16)
    for c in range(wg_ref.shape[1] // chunk):
        cols = slice(c * chunk, (c + 1) * chunk)
        g = _dot(h, wg_ref[:, cols]) + gb_ref[:, cols]
        gate_ref[:, cols] = _sigmoid(g).astype(BF16)


def _inproj(x2, nw, wqkv, wua, wub, wg, gb, qw, kw):
    t, d = x2.shape
    tm = min(ROW_TILE, t)
    row = lambda n: pl.BlockSpec((tm, n), lambda i: (i, 0))
    return pl.pallas_call(
        functools.partial(_inproj_kernel, chunk=512),
        grid=(t // tm,),
        in_specs=[row(d), _resident(nw.shape), _resident(wqkv.shape), _resident(wua.shape),
                  _resident(wub.shape), _resident(wg.shape), _resident(gb.shape),
                  _resident(qw.shape), _resident(kw.shape)],
        out_specs=[row(wqkv.shape[1]), row(wua.shape[1]), row(wg.shape[1])],
        out_shape=[jax.ShapeDtypeStruct((t, wqkv.shape[1]), BF16),
                   jax.ShapeDtypeStruct((t, wua.shape[1]), BF16),
                   jax.ShapeDtypeStruct((t, wg.shape[1]), BF16)],
        compiler_params=pltpu.CompilerParams(dimension_semantics=("arbitrary",),
                                             vmem_limit_bytes=VMEM_LIMIT),
        name="inproj",
    )(x2, nw, wqkv, wua, wub, wg, gb, qw, kw)


def _bias_tiles_kernel(tab_ref, out_ref):
    h = pl.program_id(0)
    tile = out_ref.shape[-1]
    key = lax.broadcasted_iota(jnp.int32, (tile, tile), 0)
    qry = lax.broadcasted_iota(jnp.int32, (tile, tile), 1)
    for d in range(2):
        rel = qry - key + d * tile
        bias = jnp.full((tile, tile), tab_ref[0, h], F32)
        for b in range(1, NUM_BUCKETS):
            bias = jnp.where(rel >= _BUCKET_STARTS[b], tab_ref[b, h], bias)
        bias = bias * LOG2E
        if d == 0:
            bias = jnp.where(rel >= 0, bias, -jnp.inf)
        out_ref[0, d] = bias


def _bias_tiles(rel_bias):
    return pl.pallas_call(
        _bias_tiles_kernel,
        grid=(N_HEADS,),
        in_specs=[pl.BlockSpec(memory_space=pltpu.SMEM)],
        out_specs=pl.BlockSpec((1, 2, ATT_TILE, ATT_TILE), lambda h: (h, 0, 0, 0)),
        out_shape=jax.ShapeDtypeStruct((N_HEADS, 2, ATT_TILE, ATT_TILE), F32),
        name="t5_bias_tiles",
    )(rel_bias)


def _attn_kernel(tab_ref, q_ref, k_ref, v_ref, bias_ref, lq1_ref, lk1_ref, lq2_ref, lk2_ref, sw_ref,
                 o_ref, vt_scr, acc_scr, m_scr, l_scr, *, lam_init):
    tile = ATT_TILE
    n_tiles = q_ref.shape[1] // tile
    h = pl.program_id(1)
    far_bias = tab_ref[NUM_BUCKETS - 1, h] * LOG2E

    lam = (jnp.exp(jnp.sum(lq1_ref[...] * lk1_ref[...], axis=-1, keepdims=True))
           - jnp.exp(jnp.sum(lq2_ref[...] * lk2_ref[...], axis=-1, keepdims=True)) + lam_init)
    out_gain = sw_ref[...] * (1.0 - lam_init)

    for j in range(n_tiles):
        vt_scr[j] = v_ref[0, j * tile:(j + 1) * tile, :].astype(F32).T.astype(BF16)

    lo = lax.broadcasted_iota(jnp.int32, (tile, LANES), 1) < HEAD_DIM

    def q_tile(qi, _):
        q = q_ref[0, pl.ds(pl.multiple_of(qi * tile, tile), tile), :]
        zero = jnp.zeros_like(q)
        qcat = jnp.concatenate([jnp.where(lo, q, zero), jnp.where(lo, zero, q)], axis=0)
        m_scr[...] = jnp.full(m_scr.shape, -jnp.inf, F32)
        l_scr[...] = jnp.zeros(l_scr.shape, F32)
        acc_scr[...] = jnp.zeros(acc_scr.shape, F32)

        def kv_step(j, bias_tile):
            k = k_ref[0, pl.ds(pl.multiple_of(j * tile, tile), tile), :]
            s = lax.dot_general(k, qcat, (((1,), (1,)), ((), ())), preferred_element_type=F32)
            m_old = m_scr[...]
            if bias_tile is None:
                m_new = jnp.maximum(m_old, jnp.max(s, axis=0, keepdims=True) + far_bias)
                shift = m_new - far_bias
            else:
                bt = bias_ref[0, bias_tile]
                s = s + jnp.concatenate([bt, bt], axis=1)
                m_new = jnp.maximum(m_old, jnp.max(s, axis=0, keepdims=True))
                shift = m_new
            alpha = jnp.exp2(m_old - m_new)
            p = jnp.exp2(s - shift)
            l_scr[...] = alpha * l_scr[...] + jnp.sum(p, axis=0, keepdims=True)
            acc_scr[...] = alpha * acc_scr[...] + _dot(vt_scr[j], p.astype(BF16))
            m_scr[...] = m_new

        def far_step(j, c):
            kv_step(j, None)
            return c

        lax.fori_loop(0, jnp.maximum(qi - 1, 0), far_step, 0)

        @pl.when(qi >= 1)
        def _():
            kv_step(qi - 1, 1)

        kv_step(qi, 0)

        o = acc_scr[...] / l_scr[...]
        ot = o[:, :tile] - lam * o[:, tile:]
        ot = ot * lax.rsqrt(jnp.mean(ot * ot, axis=0, keepdims=True) + EPS)
        o_ref[0, pl.ds(pl.multiple_of(qi * tile, tile), tile), :] = (ot.T * out_gain).astype(o_ref.dtype)
        return 0

    lax.fori_loop(0, n_tiles, q_tile, 0)


def _attention(qkv3, bias_tiles, rel_bias, lq1, lk1, lq2, lk2, sw, lam_init):
    b, s, _ = qkv3.shape
    tile = ATT_TILE
    head = lambda off: pl.BlockSpec((1, s, V_DIM), lambda bi, hi: (bi, 0, off + hi))
    small = lambda a: pl.BlockSpec(a.shape, lambda bi, hi: (0,) * a.ndim)
    return pl.pallas_call(
        functools.partial(_attn_kernel, lam_init=lam_init),
        grid=(b, N_HEADS),
        in_specs=[pl.BlockSpec(memory_space=pltpu.SMEM),
                  head(0), head(N_HEADS), head(2 * N_HEADS),
                  pl.BlockSpec((1, 2, tile, tile), lambda bi, hi: (hi, 0, 0, 0)),
                  small(lq1), small(lk1), small(lq2), small(lk2), small(sw)],
        out_specs=pl.BlockSpec((1, s, V_DIM), lambda bi, hi: (bi, 0, hi)),
        out_shape=jax.ShapeDtypeStruct((b, s, N_HEADS * V_DIM), BF16),
        scratch_shapes=[pltpu.VMEM((s // tile, V_DIM, tile), BF16),
                        pltpu.VMEM((V_DIM, 2 * tile), F32),
                        pltpu.VMEM((1, 2 * tile), F32),
                        pltpu.VMEM((1, 2 * tile), F32)],
        compiler_params=pltpu.CompilerParams(dimension_semantics=("arbitrary", "arbitrary"),
                                             vmem_limit_bytes=VMEM_LIMIT),
        name="diff_attention",
    )(rel_bias, qkv3, qkv3, qkv3, bias_tiles, lq1, lk1, lq2, lk2, sw)


def _conv_kernel(u_ref, halo_ref, w_ref, b_ref, lnw_ref, lnb_ref, o_ref, win_scr):
    ts = u_ref.shape[1]
    first = pl.program_id(1) == 0
    halo = halo_ref[0].astype(F32)
    win_scr[0:CONV_HALO, :] = jnp.where(first, jnp.zeros_like(halo), halo)
    win_scr[CONV_HALO:, :] = u_ref[0].astype(F32)
    base = CONV_HALO - (CONV_WIDTH - 1)
    for r in range(ts // CONV_CHUNK):
        r0 = r * CONV_CHUNK
        acc = jnp.broadcast_to(b_ref[...], (CONV_CHUNK, b_ref.shape[1]))
        for k in range(CONV_WIDTH):
            acc = acc + win_scr[r0 + base + k:r0 + base + k + CONV_CHUNK, :] * w_ref[k:k + 1, :]
        mu = jnp.mean(acc, axis=-1, keepdims=True)
        xc = acc - mu
        var = jnp.mean(xc * xc, axis=-1, keepdims=True)
        y = xc * lax.rsqrt(var + EPS) * lnw_ref[...] + lnb_ref[...]
        o_ref[0, r0:r0 + CONV_CHUNK, :] = (y * _sigmoid(y)).astype(o_ref.dtype)


def _conv_branch(glu3, w, b, lnw, lnb):
    bsz, s, c = glu3.shape
    ts = min(CONV_TILE, s)
    per = ts // CONV_HALO
    small = lambda a: pl.BlockSpec(a.shape, lambda bi, si: (0,) * a.ndim)
    return pl.pallas_call(
        _conv_kernel,
        grid=(bsz, s // ts),
        in_specs=[pl.BlockSpec((1, ts, c), lambda bi, si: (bi, si, 0)),
                  pl.BlockSpec((1, CONV_HALO, c), lambda bi, si: (bi, jnp.maximum(si * per - 1, 0), 0)),
                  small(w), small(b), small(lnw), small(lnb)],
        out_specs=pl.BlockSpec((1, ts, c), lambda bi, si: (bi, si, 0)),
        out_shape=jax.ShapeDtypeStruct((bsz, s, c), BF16),
        scratch_shapes=[pltpu.VMEM((ts + CONV_HALO, c), F32)],
        compiler_params=pltpu.CompilerParams(dimension_semantics=("arbitrary", "arbitrary"),
                                             vmem_limit_bytes=VMEM_LIMIT),
        name="conv_branch",
    )(glu3, glu3, w, b, lnw, lnb)


def _merge_kernel(x_ref, a_ref, c_ref, g_ref, wpa_ref, wpc_ref, wo_ref, o_ref):
    d = x_ref.shape[1]
    ya = _dot(a_ref[...], wpa_ref[...])
    yb = _dot(c_ref[...], wpc_ref[...])
    m = g_ref[:, :d].astype(F32) * ya + g_ref[:, d:].astype(F32) * yb
    o_ref[...] = x_ref[...] + _dot(m.astype(BF16), wo_ref[...])


def _merge(x2, a2, c2, g2, wpa, wpc, wo):
    t, d = x2.shape
    tm = min(ROW_TILE, t)
    row = lambda n: pl.BlockSpec((tm, n), lambda i: (i, 0))
    return pl.pallas_call(
        _merge_kernel,
        grid=(t // tm,),
        in_specs=[row(d), row(a2.shape[1]), row(c2.shape[1]), row(g2.shape[1]),
                  _resident(wpa.shape), _resident(wpc.shape), _resident(wo.shape)],
        out_specs=row(d),
        out_shape=jax.ShapeDtypeStruct((t, d), F32),
        compiler_params=pltpu.CompilerParams(dimension_semantics=("arbitrary",),
                                             vmem_limit_bytes=VMEM_LIMIT),
        name="merge_outproj",
    )(x2, a2, c2, g2, wpa, wpc, wo)


def _ffn_kernel(x_ref, nw_ref, wgu_ref, wd_ref, o_ref, act_scr):
    ff = wd_ref.shape[0]
    x = x_ref[...]
    h = _rms_rows(x, nw_ref[...]).astype(BF16)
    for c in range(ff // FF_CHUNK):
        g = _dot(h, wgu_ref[:, c * FF_CHUNK:(c + 1) * FF_CHUNK])
        u = _dot(h, wgu_ref[:, ff + c * FF_CHUNK:ff + (c + 1) * FF_CHUNK])
        act_scr[:, c * FF_CHUNK:(c + 1) * FF_CHUNK] = (g * _sigmoid(g) * u).astype(BF16)
    o_ref[...] = x + _dot(act_scr[...], wd_ref[...])


def _ffn(x2, nw, wgu, wd):
    t, d = x2.shape
    tm = min(ROW_TILE, t)
    row = pl.BlockSpec((tm, d), lambda i: (i, 0))
    return pl.pallas_call(
        _ffn_kernel,
        grid=(t // tm,),
        in_specs=[row, _resident(nw.shape), _resident(wgu.shape), _resident(wd.shape)],
        out_specs=row,
        out_shape=jax.ShapeDtypeStruct((t, d), F32),
        scratch_shapes=[pltpu.VMEM((tm, wd.shape[0]), BF16)],
        compiler_params=pltpu.CompilerParams(dimension_semantics=("arbitrary",),
                                             vmem_limit_bytes=VMEM_LIMIT),
        name="swiglu_ffn",
    )(x2, nw, wgu, wd)


def kernel(x, attn_norm_w, w_in, q_norm_w, k_norm_w, lam_q1, lam_k1, lam_q2, lam_k2, subln_w, w_proj_attn,
           conv_w, conv_b, conv_ln_w, conv_ln_b, w_proj_conv, gate_b, w_out, ffn_norm_w, w_gate_up, w_down,
           rel_bias):
    bsz, s, d = x.shape
    depth = w_in.shape[0]
    d_attn = N_HEADS * V_DIM
    assert s % ATT_TILE == 0 and s % CONV_TILE == 0 and (bsz * s) % ROW_TILE == 0
    assert w_down.shape[1] % FF_CHUNK == 0 and d == d_attn

    row = lambda v: v.reshape(1, -1).astype(F32)
    two = lambda v: jnp.concatenate([v, v]).reshape(1, -1).astype(F32)
    bias_tiles = _bias_tiles(rel_bias.astype(F32))
    x2 = x.reshape(bsz * s, d).astype(F32)

    for l in range(depth):
        lam_init = 0.8 - 0.6 * math.exp(-0.3 * l)
        w = w_in[l].astype(BF16)
        wqkv, wua, wub, wg = (w[:, :3 * d_attn], w[:, 3 * d_attn:3 * d_attn + d],
                              w[:, 3 * d_attn + d:3 * d_attn + 2 * d], w[:, 3 * d_attn + 2 * d:])
        qkv, glu, gates = _inproj(x2, row(attn_norm_w[l]), wqkv, wua, wub, wg, row(gate_b[l]),
                                  two(q_norm_w[l]), two(k_norm_w[l]))
        a = _attention(qkv.reshape(bsz, s, 3 * d_attn), bias_tiles, rel_bias.astype(F32),
                       row(lam_q1[l]), row(lam_k1[l]), row(lam_q2[l]), row(lam_k2[l]), row(subln_w[l]), lam_init)
        c = _conv_branch(glu.reshape(bsz, s, d), conv_w[l].astype(F32), row(conv_b[l]),
                         row(conv_ln_w[l]), row(conv_ln_b[l]))
        x2 = _merge(x2, a.reshape(bsz * s, d_attn), c.reshape(bsz * s, d), gates,
                    w_proj_attn[l].astype(BF16), w_proj_conv[l].astype(BF16), w_out[l].astype(BF16))
        x2 = _ffn(x2, row(ffn_norm_w[l]), w_gate_up[l].astype(BF16), w_down[l].astype(BF16))
    return x2.reshape(bsz, s, d).astype(x.dtype)
```

```python
import functools
import math

import numpy as np
import jax
import jax.numpy as jnp
from jax import lax
from jax.experimental import pallas as pl
from jax.experimental.pallas import tpu as pltpu

N_HEADS = 8
HEAD_DIM = 64
V_DIM = 2 * HEAD_DIM
CONV_WIDTH = 31
NUM_BUCKETS = 32
MAX_DISTANCE = 128
EPS = 1e-6
LOG2E = math.log2(math.e)

LANES = 128
ROW_TILE = 512
ATT_TILE = 256
ATT_UNROLL = 6
CONV_TILE = 256
CONV_CHUNK = 32
CONV_HALO = 32
FF_CHUNK = 256
VMEM_LIMIT = 56 * 1024 * 1024

BF16 = jnp.bfloat16
F32 = jnp.float32


def _t5_bucket_starts():
    rel = np.arange(0, 4 * MAX_DISTANCE, dtype=np.int32)
    max_exact = NUM_BUCKETS // 2
    nf = np.maximum(rel, 1).astype(np.float32)
    large = max_exact + (np.log(nf / np.float32(max_exact)) / np.float32(math.log(MAX_DISTANCE / max_exact))
                         * np.float32(NUM_BUCKETS - max_exact)).astype(np.int32)
    large = np.minimum(large, NUM_BUCKETS - 1)
    bucket = np.where(rel < max_exact, rel, large)
    starts = [int(np.argmax(bucket == b)) for b in range(NUM_BUCKETS)]
    assert all(bucket[s] == b for b, s in enumerate(starts)) and starts == sorted(starts)
    return starts


_BUCKET_STARTS = _t5_bucket_starts()
assert _BUCKET_STARTS[-1] <= ATT_TILE // 2


def _resident(shape):
    return pl.BlockSpec(shape, lambda *_: (0,) * len(shape), pipeline_mode=pl.Buffered(1))


def _dot(a, b):
    return jnp.dot(a, b, preferred_element_type=F32)


def _sigmoid(x):
    return 1.0 / (1.0 + jnp.exp(-x))


def _rms_rows(x, w):
    ms = jnp.mean(x * x, axis=-1, keepdims=True)
    return x * lax.rsqrt(ms + EPS) * w


def _head_rms(acc, w128):
    rows = acc.shape[0]
    lo = lax.broadcasted_iota(jnp.int32, (rows, LANES), 1) < HEAD_DIM
    outs = []
    for hh in range(acc.shape[1] // LANES):
        blk = acc[:, hh * LANES:(hh + 1) * LANES]
        sq = blk * blk
        s_lo = jnp.sum(jnp.where(lo, sq, 0.0), axis=-1, keepdims=True)
        s_hi = jnp.sum(jnp.where(lo, 0.0, sq), axis=-1, keepdims=True)
        r = jnp.where(lo, lax.rsqrt(s_lo / HEAD_DIM + EPS), lax.rsqrt(s_hi / HEAD_DIM + EPS))
        outs.append(blk * r * w128)
    return jnp.concatenate(outs, axis=1)


def _inproj_kernel(x_ref, nw_ref, wqkv_ref, wua_ref, wub_ref, wg_ref, gb_ref, qw_ref, kw_ref,
                   qkv_ref, glu_ref, gate_ref, *, chunk):
    d_attn = N_HEADS * V_DIM
    h = _rms_rows(x_ref[...], nw_ref[...]).astype(BF16)
    qw = qw_ref[...] * (HEAD_DIM ** -0.5 * LOG2E)
    kw = kw_ref[...]
    for c in range(3 * d_attn // chunk):
        cols = slice(c * chunk, (c + 1) * chunk)
        acc = _dot(h, wqkv_ref[:, cols])
        if c * chunk < d_attn:
            acc = _head_rms(acc, qw)
        elif c * chunk < 2 * d_attn:
            acc = _head_rms(acc, kw)
        qkv_ref[:, cols] = acc.astype(BF16)
    for c in range(wua_ref.shape[1] // chunk):
        cols = slice(c * chunk, (c + 1) * chunk)
        ua = _dot(h, wua_ref[:, cols])
        ub = _dot(h, wub_ref[:, cols])
        glu_ref[:, cols] = (ua * _sigmoid(ub)).astype(BF16)
    for c in range(wg_ref.shape[1] // chunk):
        cols = slice(c * chunk, (c + 1) * chunk)
        g = _dot(h, wg_ref[:, cols]) + gb_ref[:, cols]
        gate_ref[:, cols] = _sigmoid(g).astype(BF16)


def _inproj(x2, nw, wqkv, wua, wub, wg, gb, qw, kw):
    t, d = x2.shape
    tm = min(ROW_TILE, t)
    row = lambda n: pl.BlockSpec((tm, n), lambda i: (i, 0))
    return pl.pallas_call(
        functools.partial(_inproj_kernel, chunk=512),
        grid=(t // tm,),
        in_specs=[row(d), _resident(nw.shape), _resident(wqkv.shape), _resident(wua.shape),
                  _resident(wub.shape), _resident(wg.shape), _resident(gb.shape),
                  _resident(qw.shape), _resident(kw.shape)],
        out_specs=[row(wqkv.shape[1]), row(wua.shape[1]), row(wg.shape[1])],
        out_shape=[jax.ShapeDtypeStruct((t, wqkv.shape[1]), BF16),
                   jax.ShapeDtypeStruct((t, wua.shape[1]), BF16),
                   jax.ShapeDtypeStruct((t, wg.shape[1]), BF16)],
        compiler_params=pltpu.CompilerParams(dimension_semantics=("arbitrary",),
                                             vmem_limit_bytes=VMEM_LIMIT),
        name="inproj",
    )(x2, nw, wqkv, wua, wub, wg, gb, qw, kw)


def _bias_tiles_kernel(tab_ref, out_ref):
    h = pl.program_id(0)
    tile = out_ref.shape[-1]
    key = lax.broadcasted_iota(jnp.int32, (tile, tile), 0)
    qry = lax.broadcasted_iota(jnp.int32, (tile, tile), 1)
    far = tab_ref[NUM_BUCKETS - 1, h]
    for d in range(3):
        rel = qry - key + d * tile
        bias = jnp.full((tile, tile), tab_ref[0, h], F32)
        for b in range(1, NUM_BUCKETS):
            bias = jnp.where(rel >= _BUCKET_STARTS[b], tab_ref[b, h], bias)
        bias = (bias - far) * LOG2E
        if d == 0:
            bias = jnp.where(rel >= 0, bias, -jnp.inf)
        out_ref[0, d] = bias


def _bias_tiles(rel_bias):
    return pl.pallas_call(
        _bias_tiles_kernel,
        grid=(N_HEADS,),
        in_specs=[pl.BlockSpec(memory_space=pltpu.SMEM)],
        out_specs=pl.BlockSpec((1, 3, ATT_TILE, ATT_TILE), lambda h: (h, 0, 0, 0)),
        out_shape=jax.ShapeDtypeStruct((N_HEADS, 3, ATT_TILE, ATT_TILE), F32),
        name="t5_bias_tiles",
    )(rel_bias)


V_ROWS = V_DIM + 16


def _attn_kernel(q_ref, k_ref, v_ref, bias_ref, lq1_ref, lk1_ref, lq2_ref, lk2_ref, sw_ref, o_ref,
                 qcat_scr, vt_scr, s_scr, p_scr, alpha_scr, acc_scr, m_scr, *, lam_init):
    tile = ATT_TILE
    n_tiles = q_ref.shape[1] // tile
    n_pairs = n_tiles * (n_tiles + 1) // 2
    assert n_tiles >= 2 and ATT_UNROLL % 2 == 0 and (n_pairs + 2) % ATT_UNROLL == 0

    lam = (jnp.exp(jnp.sum(lq1_ref[...] * lk1_ref[...], axis=-1, keepdims=True))
           - jnp.exp(jnp.sum(lq2_ref[...] * lk2_ref[...], axis=-1, keepdims=True)) + lam_init)
    out_gain = sw_ref[...] * (1.0 - lam_init)

    lo = lax.broadcasted_iota(jnp.int32, (tile, LANES), 1) < HEAD_DIM
    for j in range(n_tiles):
        rows = slice(j * tile, (j + 1) * tile)
        vt_scr[j, 0:V_DIM, :] = v_ref[0, rows, :].astype(F32).T.astype(BF16)
        vt_scr[j, V_DIM:, :] = jnp.ones((V_ROWS - V_DIM, tile), BF16)
        q = q_ref[0, rows, :]
        zero = jnp.zeros_like(q)
        qcat_scr[j] = jnp.concatenate([jnp.where(lo, q, zero), jnp.where(lo, zero, q)], axis=0)
    s_scr[...] = jnp.zeros(s_scr.shape, F32)
    p_scr[...] = jnp.zeros(p_scr.shape, BF16)
    alpha_scr[...] = jnp.zeros(alpha_scr.shape, F32)
    acc_scr[...] = jnp.zeros(acc_scr.shape, F32)
    m_scr[...] = jnp.zeros(m_scr.shape, F32)

    def half_step(slot, carry):
        qa, ja, qb, jb, qc, jc = carry
        other = 1 - slot
        acc_scr[qc] = alpha_scr[slot] * acc_scr[qc] + _dot(vt_scr[jc], p_scr[slot])
        s = s_scr[other]
        m_old = jnp.where(jb == 0, -jnp.inf, m_scr[...])
        m_new = jnp.maximum(m_old, jnp.max(s, axis=0, keepdims=True))
        alpha_scr[other] = jnp.exp2(m_old - m_new)
        p_scr[other] = jnp.exp2(s - m_new).astype(BF16)
        m_scr[...] = m_new
        k = k_ref[0, pl.ds(pl.multiple_of(ja * tile, tile), tile), :]
        sa = lax.dot_general(k, qcat_scr[qa], (((1,), (1,)), ((), ())), preferred_element_type=F32)
        bt = bias_ref[0, jnp.minimum(qa - ja, 2)]
        s_scr[slot, :, :tile] = sa[:, :tile] + bt
        s_scr[slot, :, tile:] = sa[:, tile:] + bt

        wrap = ja == qa
        at_end = jnp.logical_and(wrap, qa == n_tiles - 1)
        qa2 = jnp.where(jnp.logical_and(wrap, jnp.logical_not(at_end)), qa + 1, qa)
        ja2 = jnp.where(at_end, ja, jnp.where(wrap, 0, ja + 1))
        return qa2, ja2, qa, ja, qb, jb

    def steps(_, carry):
        for u in range(ATT_UNROLL):
            carry = half_step(u % 2, carry)
        return carry

    zero, one = jnp.int32(0), jnp.int32(1)
    lax.fori_loop(0, (n_pairs + 2) // ATT_UNROLL, steps, (zero, zero, zero, one, zero, one))

    def finalize(qi, c):
        acc = acc_scr[qi]
        o = acc[:V_DIM, :] / acc[V_DIM:V_DIM + 1, :]
        ot = o[:, :tile] - lam * o[:, tile:]
        ot = ot * lax.rsqrt(jnp.mean(ot * ot, axis=0, keepdims=True) + EPS)
        o_ref[0, pl.ds(pl.multiple_of(qi * tile, tile), tile), :] = (ot.T * out_gain).astype(o_ref.dtype)
        return c

    lax.fori_loop(0, n_tiles, finalize, 0)


def _attention(qkv3, bias_tiles, lq1, lk1, lq2, lk2, sw, lam_init):
    b, s, _ = qkv3.shape
    tile = ATT_TILE
    n_tiles = s // tile
    head = lambda off: pl.BlockSpec((1, s, V_DIM), lambda bi, hi: (bi, 0, off + hi))
    small = lambda a: pl.BlockSpec(a.shape, lambda bi, hi: (0,) * a.ndim)
    return pl.pallas_call(
        functools.partial(_attn_kernel, lam_init=lam_init),
        grid=(b, N_HEADS),
        in_specs=[head(0), head(N_HEADS), head(2 * N_HEADS),
                  pl.BlockSpec((1, 3, tile, tile), lambda bi, hi: (hi, 0, 0, 0)),
                  small(lq1), small(lk1), small(lq2), small(lk2), small(sw)],
        out_specs=pl.BlockSpec((1, s, V_DIM), lambda bi, hi: (bi, 0, hi)),
        out_shape=jax.ShapeDtypeStruct((b, s, N_HEADS * V_DIM), BF16),
        scratch_shapes=[pltpu.VMEM((n_tiles, 2 * tile, V_DIM), BF16),
                        pltpu.VMEM((n_tiles, V_ROWS, tile), BF16),
                        pltpu.VMEM((2, tile, 2 * tile), F32),
                        pltpu.VMEM((2, tile, 2 * tile), BF16),
                        pltpu.VMEM((2, 1, 2 * tile), F32),
                        pltpu.VMEM((n_tiles, V_ROWS, 2 * tile), F32),
                        pltpu.VMEM((1, 2 * tile), F32)],
        compiler_params=pltpu.CompilerParams(dimension_semantics=("arbitrary", "arbitrary"),
                                             vmem_limit_bytes=VMEM_LIMIT),
        name="diff_attention",
    )(qkv3, qkv3, qkv3, bias_tiles, lq1, lk1, lq2, lk2, sw)


def _conv_kernel(u_ref, halo_ref, w_ref, b_ref, lnw_ref, lnb_ref, o_ref, win_scr):
    ts = u_ref.shape[1]
    first = pl.program_id(1) == 0
    halo = halo_ref[0].astype(F32)
    win_scr[0:CONV_HALO, :] = jnp.where(first, jnp.zeros_like(halo), halo)
    win_scr[CONV_HALO:, :] = u_ref[0].astype(F32)
    base = CONV_HALO - (CONV_WIDTH - 1)
    for r in range(ts // CONV_CHUNK):
        r0 = r * CONV_CHUNK
        acc = jnp.broadcast_to(b_ref[...], (CONV_CHUNK, b_ref.shape[1]))
        for k in range(CONV_WIDTH):
            acc = acc + win_scr[r0 + base + k:r0 + base + k + CONV_CHUNK, :] * w_ref[k:k + 1, :]
        mu = jnp.mean(acc, axis=-1, keepdims=True)
        xc = acc - mu
        var = jnp.mean(xc * xc, axis=-1, keepdims=True)
        y = xc * lax.rsqrt(var + EPS) * lnw_ref[...] + lnb_ref[...]
        o_ref[0, r0:r0 + CONV_CHUNK, :] = (y * _sigmoid(y)).astype(o_ref.dtype)


def _conv_branch(glu3, w, b, lnw, lnb):
    bsz, s, c = glu3.shape
    ts = min(CONV_TILE, s)
    per = ts // CONV_HALO
    small = lambda a: pl.BlockSpec(a.shape, lambda bi, si: (0,) * a.ndim)
    return pl.pallas_call(
        _conv_kernel,
        grid=(bsz, s // ts),
        in_specs=[pl.BlockSpec((1, ts, c), lambda bi, si: (bi, si, 0)),
                  pl.BlockSpec((1, CONV_HALO, c), lambda bi, si: (bi, jnp.maximum(si * per - 1, 0), 0)),
                  small(w), small(b), small(lnw), small(lnb)],
        out_specs=pl.BlockSpec((1, ts, c), lambda bi, si: (bi, si, 0)),
        out_shape=jax.ShapeDtypeStruct((bsz, s, c), BF16),
        scratch_shapes=[pltpu.VMEM((ts + CONV_HALO, c), F32)],
        compiler_params=pltpu.CompilerParams(dimension_semantics=("arbitrary", "arbitrary"),
                                             vmem_limit_bytes=VMEM_LIMIT),
        name="conv_branch",
    )(glu3, glu3, w, b, lnw, lnb)


def _merge_kernel(x_ref, a_ref, c_ref, g_ref, wpa_ref, wpc_ref, wo_ref, o_ref):
    d = x_ref.shape[1]
    ya = _dot(a_ref[...], wpa_ref[...])
    yb = _dot(c_ref[...], wpc_ref[...])
    m = g_ref[:, :d].astype(F32) * ya + g_ref[:, d:].astype(F32) * yb
    o_ref[...] = x_ref[...] + _dot(m.astype(BF16), wo_ref[...])


def _merge(x2, a2, c2, g2, wpa, wpc, wo):
    t, d = x2.shape
    tm = min(ROW_TILE, t)
    row = lambda n: pl.BlockSpec((tm, n), lambda i: (i, 0))
    return pl.pallas_call(
        _merge_kernel,
        grid=(t // tm,),
        in_specs=[row(d), row(a2.shape[1]), row(c2.shape[1]), row(g2.shape[1]),
                  _resident(wpa.shape), _resident(wpc.shape), _resident(wo.shape)],
        out_specs=row(d),
        out_shape=jax.ShapeDtypeStruct((t, d), F32),
        compiler_params=pltpu.CompilerParams(dimension_semantics=("arbitrary",),
                                             vmem_limit_bytes=VMEM_LIMIT),
        name="merge_outproj",
    )(x2, a2, c2, g2, wpa, wpc, wo)


def _ffn_kernel(x_ref, nw_ref, wgu_ref, wd_ref, o_ref, act_scr):
    ff = wd_ref.shape[0]
    x = x_ref[...]
    h = _rms_rows(x, nw_ref[...]).astype(BF16)
    for c in range(ff // FF_CHUNK):
        g = _dot(h, wgu_ref[:, c * FF_CHUNK:(c + 1) * FF_CHUNK])
        u = _dot(h, wgu_ref[:, ff + c * FF_CHUNK:ff + (c + 1) * FF_CHUNK])
        act_scr[:, c * FF_CHUNK:(c + 1) * FF_CHUNK] = (g * _sigmoid(g) * u).astype(BF16)
    o_ref[...] = x + _dot(act_scr[...], wd_ref[...])


def _ffn(x2, nw, wgu, wd):
    t, d = x2.shape
    tm = min(ROW_TILE, t)
    row = pl.BlockSpec((tm, d), lambda i: (i, 0))
    return pl.pallas_call(
        _ffn_kernel,
        grid=(t // tm,),
        in_specs=[row, _resident(nw.shape), _resident(wgu.shape), _resident(wd.shape)],
        out_specs=row,
        out_shape=jax.ShapeDtypeStruct((t, d), F32),
        scratch_shapes=[pltpu.VMEM((tm, wd.shape[0]), BF16)],
        compiler_params=pltpu.CompilerParams(dimension_semantics=("arbitrary",),
                                             vmem_limit_bytes=VMEM_LIMIT),
        name="swiglu_ffn",
    )(x2, nw, wgu, wd)


def kernel(x, attn_norm_w, w_in, q_norm_w, k_norm_w, lam_q1, lam_k1, lam_q2, lam_k2, subln_w, w_proj_attn,
           conv_w, conv_b, conv_ln_w, conv_ln_b, w_proj_conv, gate_b, w_out, ffn_norm_w, w_gate_up, w_down,
           rel_bias):
    bsz, s, d = x.shape
    depth = w_in.shape[0]
    d_attn = N_HEADS * V_DIM
    assert s % ATT_TILE == 0 and s % CONV_TILE == 0 and (bsz * s) % ROW_TILE == 0
    assert w_down.shape[1] % FF_CHUNK == 0 and d == d_attn

    row = lambda v: v.reshape(1, -1).astype(F32)
    two = lambda v: jnp.concatenate([v, v]).reshape(1, -1).astype(F32)
    bias_tiles = _bias_tiles(rel_bias.astype(F32))
    x2 = x.reshape(bsz * s, d).astype(F32)

    for l in range(depth):
        lam_init = 0.8 - 0.6 * math.exp(-0.3 * l)
        w = w_in[l].astype(BF16)
        wqkv, wua, wub, wg = (w[:, :3 * d_attn], w[:, 3 * d_attn:3 * d_attn + d],
                              w[:, 3 * d_attn + d:3 * d_attn + 2 * d], w[:, 3 * d_attn + 2 * d:])
        qkv, glu, gates = _inproj(x2, row(attn_norm_w[l]), wqkv, wua, wub, wg, row(gate_b[l]),
                                  two(q_norm_w[l]), two(k_norm_w[l]))
        a = _attention(qkv.reshape(bsz, s, 3 * d_attn), bias_tiles,
                       row(lam_q1[l]), row(lam_k1[l]), row(lam_q2[l]), row(lam_k2[l]), row(subln_w[l]), lam_init)
        c = _conv_branch(glu.reshape(bsz, s, d), conv_w[l].astype(F32), row(conv_b[l]),
                         row(conv_ln_w[l]), row(conv_ln_b[l]))
        x2 = _merge(x2, a.reshape(bsz * s, d_attn), c.reshape(bsz * s, d), gates,
                    w_proj_attn[l].astype(BF16), w_proj_conv[l].astype(BF16), w_out[l].astype(BF16))
        x2 = _ffn(x2, row(ffn_norm_w[l]), w_gate_up[l].astype(BF16), w_down[l].astype(BF16))
    return x2.reshape(bsz, s, d).astype(x.dtype)
```

```python
import functools
import math

import numpy as np
import jax
import jax.numpy as jnp
from jax import lax
from jax.experimental import pallas as pl
from jax.experimental.pallas import tpu as pltpu

N_HEADS = 8
HEAD_DIM = 64
V_DIM = 2 * HEAD_DIM
CONV_WIDTH = 31
NUM_BUCKETS = 32
MAX_DISTANCE = 128
EPS = 1e-6
LOG2E = math.log2(math.e)
Q_GAIN_SCALE = HEAD_DIM ** -0.5 * LOG2E

LANES = 128
ROW_TILE = 512
ATT_TILE = 256
ATT_UNROLL = 6
ATT_FAST_UNROLL = 12
ATT_FINAL_UNROLL = 4
ATT_NO_MAX_BOUND = 60.0
ATT_BOUND_SLACK = 1.02
CONV_TILE = 256
CONV_CHUNK = 64
CONV_LN_ROWS = 32
CONV_HALO = 32
FF_CHUNK = 256
VMEM_LIMIT = 56 * 1024 * 1024

BF16 = jnp.bfloat16
F32 = jnp.float32


def _t5_bucket_starts():
    rel = np.arange(0, 4 * MAX_DISTANCE, dtype=np.int32)
    max_exact = NUM_BUCKETS // 2
    nf = np.maximum(rel, 1).astype(np.float32)
    large = max_exact + (np.log(nf / np.float32(max_exact)) / np.float32(math.log(MAX_DISTANCE / max_exact))
                         * np.float32(NUM_BUCKETS - max_exact)).astype(np.int32)
    large = np.minimum(large, NUM_BUCKETS - 1)
    bucket = np.where(rel < max_exact, rel, large)
    starts = [int(np.argmax(bucket == b)) for b in range(NUM_BUCKETS)]
    assert all(bucket[s] == b for b, s in enumerate(starts)) and starts == sorted(starts)
    return starts


_BUCKET_STARTS = _t5_bucket_starts()
assert _BUCKET_STARTS[-1] <= ATT_TILE // 2


def _resident(shape):
    return pl.BlockSpec(shape, lambda *_: (0,) * len(shape), pipeline_mode=pl.Buffered(1))


def _dot(a, b):
    return jnp.dot(a, b, preferred_element_type=F32)


def _sigmoid(x):
    return 1.0 / (1.0 + jnp.exp(-x))


def _rms_rows(x, w):
    ms = jnp.mean(x * x, axis=-1, keepdims=True)
    return x * lax.rsqrt(ms + EPS) * w


def _head_rms(acc, w128):
    rows = acc.shape[0]
    lo = lax.broadcasted_iota(jnp.int32, (rows, LANES), 1) < HEAD_DIM
    outs = []
    for hh in range(acc.shape[1] // LANES):
        blk = acc[:, hh * LANES:(hh + 1) * LANES]
        sq = blk * blk
        s_lo = jnp.sum(jnp.where(lo, sq, 0.0), axis=-1, keepdims=True)
        s_hi = jnp.sum(jnp.where(lo, 0.0, sq), axis=-1, keepdims=True)
        r = jnp.where(lo, lax.rsqrt(s_lo / HEAD_DIM + EPS), lax.rsqrt(s_hi / HEAD_DIM + EPS))
        outs.append(blk * r * w128)
    return jnp.concatenate(outs, axis=1)


def _inproj_kernel(x_ref, nw_ref, wqkv_ref, wua_ref, wub_ref, wg_ref, gb_ref, qw_ref, kw_ref,
                   qqk_ref, vt_ref, glu_ref, gate_ref, *, chunk):
    d_attn = N_HEADS * V_DIM
    h = _rms_rows(x_ref[...], nw_ref[...]).astype(BF16)
    qw = qw_ref[...] * Q_GAIN_SCALE
    kw = kw_ref[...]
    first_half = lax.broadcasted_iota(jnp.int32, (x_ref.shape[0], chunk), 1) % V_DIM < HEAD_DIM
    for c in range(3 * d_attn // chunk):
        cols = slice(c * chunk, (c + 1) * chunk)
        out_cols = slice(d_attn + c * chunk, d_attn + (c + 1) * chunk)
        acc = _dot(h, wqkv_ref[:, cols])
        if c * chunk < d_attn:
            acc = _head_rms(acc, qw)
            qqk_ref[:, cols] = jnp.where(first_half, acc, 0.0).astype(BF16)
            qqk_ref[:, out_cols] = jnp.where(first_half, 0.0, acc).astype(BF16)
        elif c * chunk < 2 * d_attn:
            qqk_ref[:, out_cols] = _head_rms(acc, kw).astype(BF16)
        else:
            for hh in range(chunk // V_DIM):
                ch = c * chunk - 2 * d_attn + hh * V_DIM
                vt_ref[0, ch:ch + V_DIM, :] = acc[:, hh * V_DIM:(hh + 1) * V_DIM].T.astype(BF16)
    for c in range(wua_ref.shape[1] // chunk):
        cols = slice(c * chunk, (c + 1) * chunk)
        ua = _dot(h, wua_ref[:, cols])
        ub = _dot(h, wub_ref[:, cols])
        glu_ref[:, cols] = (ua * _sigmoid(ub)).astype(BF16)
    for c in range(wg_ref.shape[1] // chunk):
        cols = slice(c * chunk, (c + 1) * chunk)
        g = _dot(h, wg_ref[:, cols]) + gb_ref[:, cols]
        gate_ref[:, cols] = _sigmoid(g).astype(BF16)


def _inproj(x2, seq, nw, wqkv, wua, wub, wg, gb, qw, kw):
    t, d = x2.shape
    tm = min(ROW_TILE, seq)
    per_seq = seq // tm
    d_attn = N_HEADS * V_DIM
    row = lambda n: pl.BlockSpec((tm, n), lambda i: (i, 0))
    return pl.pallas_call(
        functools.partial(_inproj_kernel, chunk=512),
        grid=(t // tm,),
        in_specs=[row(d), _resident(nw.shape), _resident(wqkv.shape), _resident(wua.shape),
                  _resident(wub.shape), _resident(wg.shape), _resident(gb.shape),
                  _resident(qw.shape), _resident(kw.shape)],
        out_specs=[row(3 * d_attn),
                   pl.BlockSpec((1, d_attn, tm), lambda i: (i // per_seq, 0, i % per_seq)),
                   row(wua.shape[1]), row(wg.shape[1])],
        out_shape=[jax.ShapeDtypeStruct((t, 3 * d_attn), BF16),
                   jax.ShapeDtypeStruct((t // seq, d_attn, seq), BF16),
                   jax.ShapeDtypeStruct((t, wua.shape[1]), BF16),
                   jax.ShapeDtypeStruct((t, wg.shape[1]), BF16)],
        compiler_params=pltpu.CompilerParams(dimension_semantics=("arbitrary",),
                                             vmem_limit_bytes=VMEM_LIMIT),
        name="inproj",
    )(x2, nw, wqkv, wua, wub, wg, gb, qw, kw)


def _bias_tiles_kernel(tab_ref, out_ref):
    h = pl.program_id(0)
    tile = out_ref.shape[-1]
    key = lax.broadcasted_iota(jnp.int32, (tile, tile), 0)
    qry = lax.broadcasted_iota(jnp.int32, (tile, tile), 1)
    far = tab_ref[NUM_BUCKETS - 1, h]
    for d in range(3):
        rel = qry - key + d * tile
        bias = jnp.full((tile, tile), tab_ref[0, h], F32)
        for b in range(1, NUM_BUCKETS):
            bias = jnp.where(rel >= _BUCKET_STARTS[b], tab_ref[b, h], bias)
        bias = (bias - far) * LOG2E
        if d == 0:
            bias = jnp.where(rel >= 0, bias, -jnp.inf)
        out_ref[0, d] = bias


def _bias_tiles(rel_bias):
    return pl.pallas_call(
        _bias_tiles_kernel,
        grid=(N_HEADS,),
        in_specs=[pl.BlockSpec(memory_space=pltpu.SMEM)],
        out_specs=pl.BlockSpec((1, 3, ATT_TILE, ATT_TILE), lambda h: (h, 0, 0, 0)),
        out_shape=jax.ShapeDtypeStruct((N_HEADS, 3, ATT_TILE, ATT_TILE), F32),
        name="t5_bias_tiles",
    )(rel_bias)


V_ROWS = V_DIM + 16


def _attn_kernel(q1_ref, q2_ref, k_ref, vt_ref, bias_ref, qw_ref, kw_ref, lq1_ref, lk1_ref, lq2_ref, lk2_ref,
                 sw_ref, o_ref, vt_scr, s_scr, p_scr, alpha_scr, acc_scr, m_scr, *, lam_init):
    tile = ATT_TILE
    n_tiles = k_ref.shape[1] // tile
    n_pairs = n_tiles * (n_tiles + 1) // 2
    assert n_tiles >= 2 and ATT_UNROLL % 2 == 0 and (n_pairs + 2) % ATT_UNROLL == 0 and ATT_FAST_UNROLL % 3 == 0

    lam = (jnp.exp(jnp.sum(lq1_ref[...] * lk1_ref[...], axis=-1, keepdims=True))
           - jnp.exp(jnp.sum(lq2_ref[...] * lk2_ref[...], axis=-1, keepdims=True)) + lam_init)
    out_gain = sw_ref[...] * (1.0 - lam_init)

    for j in range(n_tiles):
        vt_scr[j, 0:V_DIM, :] = vt_ref[0, :, j * tile:(j + 1) * tile]
        vt_scr[j, V_DIM:, :] = jnp.ones((V_ROWS - V_DIM, tile), BF16)
    p_scr[...] = jnp.zeros(p_scr.shape, BF16)
    acc_scr[...] = jnp.zeros(acc_scr.shape, F32)

    near_bias = jnp.maximum(jnp.abs(bias_ref[0, 1]),
                            jnp.where(bias_ref[0, 0] == -jnp.inf, 0.0, jnp.abs(bias_ref[0, 0])))
    score_bound = (HEAD_DIM * Q_GAIN_SCALE * ATT_BOUND_SLACK * jnp.max(jnp.abs(qw_ref[...]))
                   * jnp.max(jnp.abs(kw_ref[...])) + jnp.max(near_bias))
    no_max = score_bound <= ATT_NO_MAX_BOUND

    def advance(qa, ja):
        wrap = ja == qa
        at_end = jnp.logical_and(wrap, qa == n_tiles - 1)
        qa2 = jnp.where(jnp.logical_and(wrap, jnp.logical_not(at_end)), qa + 1, qa)
        ja2 = jnp.where(at_end, ja, jnp.where(wrap, 0, ja + 1))
        return qa2, ja2, at_end

    def scores(qa, ja):
        k = k_ref[0, pl.ds(pl.multiple_of(ja * tile, tile), tile), :]
        q_rows = pl.ds(pl.multiple_of(qa * tile, tile), tile)
        nt = (((1,), (1,)), ((), ()))
        s1 = lax.dot_general(k, q1_ref[0, q_rows, :], nt, preferred_element_type=F32)
        s2 = lax.dot_general(k, q2_ref[0, q_rows, :], nt, preferred_element_type=F32)
        bt = bias_ref[0, jnp.minimum(qa - ja, 2)]
        return s1 + bt, s2 + bt

    def fast_step(slot, carry):
        qa, ja, ia, ib, jb, ic, jc = carry
        acc_scr[ic] = acc_scr[ic] + _dot(vt_scr[jc], p_scr[(slot + 1) % 3])
        s1, s2 = scores(qa, ja)
        p_scr[slot, :, :tile] = jnp.exp2(s1).astype(BF16)
        p_scr[slot, :, tile:] = jnp.exp2(s2).astype(BF16)
        qa2, ja2, at_end = advance(qa, ja)
        ia2 = jnp.where(jnp.logical_or(at_end, ia == n_tiles), n_tiles, qa2)
        return qa2, ja2, ia2, ia, ja, ib, jb

    def fast_steps(_, carry):
        for u in range(ATT_FAST_UNROLL):
            carry = fast_step(u % 3, carry)
        return carry

    @pl.when(no_max)
    def _():
        zero, spare = jnp.int32(0), jnp.int32(n_tiles)
        lax.fori_loop(0, pl.cdiv(n_pairs + 2, ATT_FAST_UNROLL), fast_steps,
                      (zero, zero, zero, spare, zero, spare, zero))

    def half_step(slot, carry):
        qa, ja, qb, jb, qc, jc = carry
        other = 1 - slot
        acc_scr[qc] = alpha_scr[slot] * acc_scr[qc] + _dot(vt_scr[jc], p_scr[slot])
        s = s_scr[other]
        m_old = jnp.where(jb == 0, -jnp.inf, m_scr[...])
        m_new = jnp.maximum(m_old, jnp.max(s, axis=0, keepdims=True))
        alpha_scr[other] = jnp.exp2(m_old - m_new)
        p_scr[other] = jnp.exp2(s - m_new).astype(BF16)
        m_scr[...] = m_new
        s_scr[slot, :, :tile], s_scr[slot, :, tile:] = scores(qa, ja)
        qa2, ja2, _ = advance(qa, ja)
        return qa2, ja2, qa, ja, qb, jb

    def steps(_, carry):
        for u in range(ATT_UNROLL):
            carry = half_step(u % 2, carry)
        return carry

    @pl.when(jnp.logical_not(no_max))
    def _():
        s_scr[...] = jnp.zeros(s_scr.shape, F32)
        alpha_scr[...] = jnp.zeros(alpha_scr.shape, F32)
        m_scr[...] = jnp.zeros(m_scr.shape, F32)
        zero, one = jnp.int32(0), jnp.int32(1)
        lax.fori_loop(0, (n_pairs + 2) // ATT_UNROLL, steps, (zero, zero, zero, one, zero, one))

    fin_unroll = math.gcd(n_tiles, ATT_FINAL_UNROLL)

    def finalize(i, c):
        for u in range(fin_unroll):
            qi = i * fin_unroll + u
            acc = acc_scr[qi]
            o = acc[:V_DIM, :] / acc[V_DIM:V_DIM + 1, :]
            ot = o[:, :tile] - lam * o[:, tile:]
            ot = ot * lax.rsqrt(jnp.mean(ot * ot, axis=0, keepdims=True) + EPS)
            o_ref[0, pl.ds(pl.multiple_of(qi * tile, tile), tile), :] = (ot.T * out_gain).astype(o_ref.dtype)
        return c

    lax.fori_loop(0, n_tiles // fin_unroll, finalize, 0)


def _attention(qqk3, vt, bias_tiles, qw, kw, lq1, lk1, lq2, lk2, sw, lam_init):
    b, s, _ = qqk3.shape
    tile = ATT_TILE
    n_tiles = s // tile
    head = lambda off: pl.BlockSpec((1, s, V_DIM), lambda bi, hi: (bi, 0, off + hi))
    small = lambda a: pl.BlockSpec(a.shape, lambda bi, hi: (0,) * a.ndim)
    return pl.pallas_call(
        functools.partial(_attn_kernel, lam_init=lam_init),
        grid=(b, N_HEADS),
        in_specs=[head(0), head(N_HEADS), head(2 * N_HEADS),
                  pl.BlockSpec((1, V_DIM, s), lambda bi, hi: (bi, hi, 0)),
                  pl.BlockSpec((1, 3, tile, tile), lambda bi, hi: (hi, 0, 0, 0)),
                  small(qw), small(kw), small(lq1), small(lk1), small(lq2), small(lk2), small(sw)],
        out_specs=pl.BlockSpec((1, s, V_DIM), lambda bi, hi: (bi, 0, hi)),
        out_shape=jax.ShapeDtypeStruct((b, s, N_HEADS * V_DIM), BF16),
        scratch_shapes=[pltpu.VMEM((n_tiles, V_ROWS, tile), BF16),
                        pltpu.VMEM((2, tile, 2 * tile), F32),
                        pltpu.VMEM((3, tile, 2 * tile), BF16),
                        pltpu.VMEM((2, 1, 2 * tile), F32),
                        pltpu.VMEM((n_tiles + 1, V_ROWS, 2 * tile), F32),
                        pltpu.VMEM((1, 2 * tile), F32)],
        compiler_params=pltpu.CompilerParams(dimension_semantics=("arbitrary", "arbitrary"),
                                             vmem_limit_bytes=VMEM_LIMIT),
        name="diff_attention",
    )(qqk3, qqk3, qqk3, vt, bias_tiles, qw, kw, lq1, lk1, lq2, lk2, sw)


def _conv_kernel(u_ref, halo_ref, w_ref, b_ref, lnw_ref, lnb_ref, o_ref, win_scr, y_scr):
    ts, n_ch = u_ref.shape[1], u_ref.shape[2]
    sub = 8
    base = CONV_HALO - (CONV_WIDTH - 1)
    first = pl.program_id(1) == 0
    halo = halo_ref[0].astype(F32)
    win_scr[0:CONV_HALO, :] = jnp.where(first, jnp.zeros_like(halo), halo)
    win_scr[CONV_HALO:CONV_HALO + ts, :] = u_ref[0].astype(F32)
    win_scr[CONV_HALO + ts:, :] = jnp.zeros((sub, n_ch), F32)
    for c in range(n_ch // LANES):
        lanes = slice(c * LANES, (c + 1) * LANES)
        for r0 in range(0, ts, CONV_CHUNK):
            y = jnp.broadcast_to(b_ref[:, lanes], (CONV_CHUNK, LANES))
            for r in range(sub):
                part = None
                for o in range(r, base + CONV_WIDTH, sub):
                    if o < base:
                        continue
                    rows = slice(r0 + o - r, r0 + o - r + CONV_CHUNK + sub)
                    term = win_scr[rows, lanes] * w_ref[o - base:o - base + 1, lanes]
                    part = term if part is None else part + term
                y = y + part[r:r + CONV_CHUNK, :]
            y_scr[r0:r0 + CONV_CHUNK, lanes] = y
    for r0 in range(0, ts, CONV_LN_ROWS):
        acc = y_scr[r0:r0 + CONV_LN_ROWS, :]
        mu = jnp.mean(acc, axis=-1, keepdims=True)
        xc = acc - mu
        var = jnp.mean(xc * xc, axis=-1, keepdims=True)
        y = xc * lax.rsqrt(var + EPS) * lnw_ref[...] + lnb_ref[...]
        o_ref[0, r0:r0 + CONV_LN_ROWS, :] = (y * _sigmoid(y)).astype(o_ref.dtype)


def _conv_branch(glu3, w, b, lnw, lnb):
    bsz, s, c = glu3.shape
    ts = min(CONV_TILE, s)
    per = ts // CONV_HALO
    small = lambda a: pl.BlockSpec(a.shape, lambda bi, si: (0,) * a.ndim)
    return pl.pallas_call(
        _conv_kernel,
        grid=(bsz, s // ts),
        in_specs=[pl.BlockSpec((1, ts, c), lambda bi, si: (bi, si, 0)),
                  pl.BlockSpec((1, CONV_HALO, c), lambda bi, si: (bi, jnp.maximum(si * per - 1, 0), 0)),
                  small(w), small(b), small(lnw), small(lnb)],
        out_specs=pl.BlockSpec((1, ts, c), lambda bi, si: (bi, si, 0)),
        out_shape=jax.ShapeDtypeStruct((bsz, s, c), BF16),
        scratch_shapes=[pltpu.VMEM((ts + CONV_HALO + 8, c), F32), pltpu.VMEM((ts, c), F32)],
        compiler_params=pltpu.CompilerParams(dimension_semantics=("arbitrary", "arbitrary"),
                                             vmem_limit_bytes=VMEM_LIMIT),
        name="conv_branch",
    )(glu3, glu3, w, b, lnw, lnb)


def _merge_kernel(x_ref, a_ref, c_ref, g_ref, wpa_ref, wpc_ref, wo_ref, o_ref):
    d = x_ref.shape[1]
    ya = _dot(a_ref[...], wpa_ref[...])
    yb = _dot(c_ref[...], wpc_ref[...])
    m = g_ref[:, :d].astype(F32) * ya + g_ref[:, d:].astype(F32) * yb
    o_ref[...] = x_ref[...] + _dot(m.astype(BF16), wo_ref[...])


def _merge(x2, a2, c2, g2, wpa, wpc, wo):
    t, d = x2.shape
    tm = min(ROW_TILE, t)
    row = lambda n: pl.BlockSpec((tm, n), lambda i: (i, 0))
    return pl.pallas_call(
        _merge_kernel,
        grid=(t // tm,),
        in_specs=[row(d), row(a2.shape[1]), row(c2.shape[1]), row(g2.shape[1]),
                  _resident(wpa.shape), _resident(wpc.shape), _resident(wo.shape)],
        out_specs=row(d),
        out_shape=jax.ShapeDtypeStruct((t, d), F32),
        compiler_params=pltpu.CompilerParams(dimension_semantics=("arbitrary",),
                                             vmem_limit_bytes=VMEM_LIMIT),
        name="merge_outproj",
    )(x2, a2, c2, g2, wpa, wpc, wo)


def _ffn_kernel(x_ref, nw_ref, wgu_ref, wd_ref, o_ref, act_scr):
    ff = wd_ref.shape[0]
    x = x_ref[...]
    h = _rms_rows(x, nw_ref[...]).astype(BF16)
    for c in range(ff // FF_CHUNK):
        g = _dot(h, wgu_ref[:, c * FF_CHUNK:(c + 1) * FF_CHUNK])
        u = _dot(h, wgu_ref[:, ff + c * FF_CHUNK:ff + (c + 1) * FF_CHUNK])
        act_scr[:, c * FF_CHUNK:(c + 1) * FF_CHUNK] = (g * _sigmoid(g) * u).astype(BF16)
    o_ref[...] = x + _dot(act_scr[...], wd_ref[...])


def _ffn(x2, nw, wgu, wd):
    t, d = x2.shape
    tm = min(ROW_TILE, t)
    row = pl.BlockSpec((tm, d), lambda i: (i, 0))
    return pl.pallas_call(
        _ffn_kernel,
        grid=(t // tm,),
        in_specs=[row, _resident(nw.shape), _resident(wgu.shape), _resident(wd.shape)],
        out_specs=row,
        out_shape=jax.ShapeDtypeStruct((t, d), F32),
        scratch_shapes=[pltpu.VMEM((tm, wd.shape[0]), BF16)],
        compiler_params=pltpu.CompilerParams(dimension_semantics=("arbitrary",),
                                             vmem_limit_bytes=VMEM_LIMIT),
        name="swiglu_ffn",
    )(x2, nw, wgu, wd)


def kernel(x, attn_norm_w, w_in, q_norm_w, k_norm_w, lam_q1, lam_k1, lam_q2, lam_k2, subln_w, w_proj_attn,
           conv_w, conv_b, conv_ln_w, conv_ln_b, w_proj_conv, gate_b, w_out, ffn_norm_w, w_gate_up, w_down,
           rel_bias):
    bsz, s, d = x.shape
    depth = w_in.shape[0]
    d_attn = N_HEADS * V_DIM
    assert s % ATT_TILE == 0 and s % CONV_TILE == 0 and (bsz * s) % ROW_TILE == 0
    assert w_down.shape[1] % FF_CHUNK == 0 and d == d_attn

    row = lambda v: v.reshape(1, -1).astype(F32)
    two = lambda v: jnp.concatenate([v, v]).reshape(1, -1).astype(F32)
    bias_tiles = _bias_tiles(rel_bias.astype(F32))
    x2 = x.reshape(bsz * s, d).astype(F32)

    for l in range(depth):
        lam_init = 0.8 - 0.6 * math.exp(-0.3 * l)
        w = w_in[l].astype(BF16)
        wqkv, wua, wub, wg = (w[:, :3 * d_attn], w[:, 3 * d_attn:3 * d_attn + d],
                              w[:, 3 * d_attn + d:3 * d_attn + 2 * d], w[:, 3 * d_attn + 2 * d:])
        qw, kw = two(q_norm_w[l]), two(k_norm_w[l])
        qqk, vt, glu, gates = _inproj(x2, s, row(attn_norm_w[l]), wqkv, wua, wub, wg, row(gate_b[l]), qw, kw)
        a = _attention(qqk.reshape(bsz, s, 3 * d_attn), vt, bias_tiles, qw, kw,
                       row(lam_q1[l]), row(lam_k1[l]), row(lam_q2[l]), row(lam_k2[l]), row(subln_w[l]), lam_init)
        c = _conv_branch(glu.reshape(bsz, s, d), conv_w[l].astype(F32), row(conv_b[l]),
                         row(conv_ln_w[l]), row(conv_ln_b[l]))
        x2 = _merge(x2, a.reshape(bsz * s, d_attn), c.reshape(bsz * s, d), gates,
                    w_proj_attn[l].astype(BF16), w_proj_conv[l].astype(BF16), w_out[l].astype(BF16))
        x2 = _ffn(x2, row(ffn_norm_w[l]), w_gate_up[l].astype(BF16), w_down[l].astype(BF16))
    return x2.reshape(bsz, s, d).astype(x.dtype)
```

```python
import functools
import math

import numpy as np
import jax
import jax.numpy as jnp
from jax import lax
from jax.experimental import pallas as pl
from jax.experimental.pallas import tpu as pltpu

N_HEADS = 8
HEAD_DIM = 64
V_DIM = 2 * HEAD_DIM
CONV_WIDTH = 31
NUM_BUCKETS = 32
MAX_DISTANCE = 128
EPS = 1e-6
LOG2E = math.log2(math.e)
Q_GAIN_SCALE = HEAD_DIM ** -0.5 * LOG2E

LANES = 128
ROW_TILE = 512
ATT_TILE = 256
ATT_UNROLL = 6
ATT_FAST_UNROLL = 12
ATT_FINAL_UNROLL = 4
ATT_NO_MAX_BOUND = 60.0
ATT_BOUND_SLACK = 1.02
CONV_CHUNK = 64
CONV_LN_ROWS = 32
CONV_HALO = 32
POST_TILE = 256
TOKEN_ROWS = 16
FF_CHUNK = 256
VMEM_LIMIT = 56 * 1024 * 1024

BF16 = jnp.bfloat16
F32 = jnp.float32


def _t5_bucket_starts():
    rel = np.arange(0, 4 * MAX_DISTANCE, dtype=np.int32)
    max_exact = NUM_BUCKETS // 2
    nf = np.maximum(rel, 1).astype(np.float32)
    large = max_exact + (np.log(nf / np.float32(max_exact)) / np.float32(math.log(MAX_DISTANCE / max_exact))
                         * np.float32(NUM_BUCKETS - max_exact)).astype(np.int32)
    large = np.minimum(large, NUM_BUCKETS - 1)
    bucket = np.where(rel < max_exact, rel, large)
    starts = [int(np.argmax(bucket == b)) for b in range(NUM_BUCKETS)]
    assert all(bucket[s] == b for b, s in enumerate(starts)) and starts == sorted(starts)
    return starts


_BUCKET_STARTS = _t5_bucket_starts()
assert _BUCKET_STARTS[-1] <= ATT_TILE // 2


def _resident(shape):
    return pl.BlockSpec(shape, lambda *_: (0,) * len(shape), pipeline_mode=pl.Buffered(1))


def _dot(a, b):
    return jnp.dot(a, b, preferred_element_type=F32)


def _sigmoid(x):
    return 1.0 / (1.0 + jnp.exp(-x))


def _zero_like_token(tok):
    bits = lax.bitcast_convert_type(tok, jnp.uint32)
    return lax.bitcast_convert_type((bits >> 16) >> 16, F32)


def _rms_rows(x, w):
    ms = jnp.mean(x * x, axis=-1, keepdims=True)
    return x * lax.rsqrt(ms + EPS) * w


def _head_rms(acc, w128):
    rows = acc.shape[0]
    lo = lax.broadcasted_iota(jnp.int32, (rows, LANES), 1) < HEAD_DIM
    outs = []
    for hh in range(acc.shape[1] // LANES):
        blk = acc[:, hh * LANES:(hh + 1) * LANES]
        sq = blk * blk
        s_lo = jnp.sum(jnp.where(lo, sq, 0.0), axis=-1, keepdims=True)
        s_hi = jnp.sum(jnp.where(lo, 0.0, sq), axis=-1, keepdims=True)
        r = jnp.where(lo, lax.rsqrt(s_lo / HEAD_DIM + EPS), lax.rsqrt(s_hi / HEAD_DIM + EPS))
        outs.append(blk * r * w128)
    return jnp.concatenate(outs, axis=1)


def _inproj_kernel(x_ref, nw_ref, wqkv_ref, wua_ref, wub_ref, wg_ref, gb_ref, qw_ref, kw_ref,
                   k_ref, qvt_ref, glu_ref, gate_ref, *, chunk):
    d_attn = N_HEADS * V_DIM
    tm = x_ref.shape[0]
    h = _rms_rows(x_ref[...], nw_ref[...]).astype(BF16)
    qw = qw_ref[...] * Q_GAIN_SCALE
    kw = kw_ref[...]
    first_half = lax.broadcasted_iota(jnp.int32, (tm, V_DIM), 1) < HEAD_DIM
    for c in range(3 * d_attn // chunk):
        cols = slice(c * chunk, (c + 1) * chunk)
        acc = _dot(h, wqkv_ref[:, cols])
        if d_attn <= c * chunk < 2 * d_attn:
            k_ref[:, c * chunk - d_attn:(c + 1) * chunk - d_attn] = _head_rms(acc, kw).astype(BF16)
            continue
        if c * chunk < d_attn:
            acc = _head_rms(acc, qw)
        for hh in range(chunk // V_DIM):
            blk = acc[:, hh * V_DIM:(hh + 1) * V_DIM]
            ch = c * chunk % d_attn + hh * V_DIM
            if c * chunk < d_attn:
                qvt_ref[0, ch:ch + V_DIM, :] = jnp.where(first_half, blk, 0.0).T.astype(BF16)
                qvt_ref[0, d_attn + ch:d_attn + ch + V_DIM, :] = jnp.where(first_half, 0.0, blk).T.astype(BF16)
            else:
                qvt_ref[0, 2 * d_attn + ch:2 * d_attn + ch + V_DIM, :] = blk.T.astype(BF16)
    for c in range(wua_ref.shape[1] // chunk):
        cols = slice(c * chunk, (c + 1) * chunk)
        ua = _dot(h, wua_ref[:, cols])
        ub = _dot(h, wub_ref[:, cols])
        glu_ref[:, cols] = (ua * _sigmoid(ub)).astype(BF16)
    for c in range(wg_ref.shape[1] // chunk):
        cols = slice(c * chunk, (c + 1) * chunk)
        g = _dot(h, wg_ref[:, cols]) + gb_ref[:, cols]
        gate_ref[:, cols] = _sigmoid(g).astype(BF16)


def _inproj(x2, seq, nw, wqkv, wua, wub, wg, gb, qw, kw):
    t, d = x2.shape
    tm = min(ROW_TILE, seq)
    per_seq = seq // tm
    d_attn = N_HEADS * V_DIM
    row = lambda n: pl.BlockSpec((tm, n), lambda i: (i, 0))
    return pl.pallas_call(
        functools.partial(_inproj_kernel, chunk=512),
        grid=(t // tm,),
        in_specs=[row(d)] + [_resident(a.shape) for a in (nw, wqkv, wua, wub, wg, gb, qw, kw)],
        out_specs=[row(d_attn),
                   pl.BlockSpec((1, 3 * d_attn, tm), lambda i: (i // per_seq, 0, i % per_seq)),
                   row(wua.shape[1]), row(wg.shape[1])],
        out_shape=[jax.ShapeDtypeStruct((t, d_attn), BF16),
                   jax.ShapeDtypeStruct((t // seq, 3 * d_attn, seq), BF16),
                   jax.ShapeDtypeStruct((t, wua.shape[1]), BF16),
                   jax.ShapeDtypeStruct((t, wg.shape[1]), BF16)],
        compiler_params=pltpu.CompilerParams(dimension_semantics=("arbitrary",),
                                             vmem_limit_bytes=VMEM_LIMIT),
        name="inproj",
    )(x2, nw, wqkv, wua, wub, wg, gb, qw, kw)


def _bias_tiles_kernel(tab_ref, out_ref):
    h = pl.program_id(0)
    tile = out_ref.shape[-1]
    key = lax.broadcasted_iota(jnp.int32, (tile, tile), 0)
    qry = lax.broadcasted_iota(jnp.int32, (tile, tile), 1)
    far = tab_ref[NUM_BUCKETS - 1, h]
    for d in range(3):
        rel = qry - key + d * tile
        bias = jnp.full((tile, tile), tab_ref[0, h], F32)
        for b in range(1, NUM_BUCKETS):
            bias = jnp.where(rel >= _BUCKET_STARTS[b], tab_ref[b, h], bias)
        bias = (bias - far) * LOG2E
        if d == 0:
            bias = jnp.where(rel >= 0, bias, -jnp.inf)
        out_ref[0, d] = bias


def _bias_tiles(rel_bias):
    return pl.pallas_call(
        _bias_tiles_kernel,
        grid=(N_HEADS,),
        in_specs=[pl.BlockSpec(memory_space=pltpu.SMEM)],
        out_specs=pl.BlockSpec((1, 3, ATT_TILE, ATT_TILE), lambda h: (h, 0, 0, 0)),
        out_shape=jax.ShapeDtypeStruct((N_HEADS, 3, ATT_TILE, ATT_TILE), F32),
        name="t5_bias_tiles",
    )(rel_bias)


V_ROWS = V_DIM + 16


def _attn_kernel(k_ref, q1t_ref, q2t_ref, vt_ref, bias_ref, qw_ref, kw_ref, lq1_ref, lk1_ref, lq2_ref, lk2_ref,
                 sw_ref, o_ref, qt_scr, vt_scr, s_scr, p_scr, alpha_scr, acc_scr, m_scr, *, lam_init):
    tile = ATT_TILE
    n_tiles = k_ref.shape[1] // tile
    n_pairs = n_tiles * (n_tiles + 1) // 2
    assert n_tiles >= 2 and ATT_UNROLL % 2 == 0 and (n_pairs + 2) % ATT_UNROLL == 0 and ATT_FAST_UNROLL % 3 == 0

    lam = (jnp.exp(jnp.sum(lq1_ref[...] * lk1_ref[...], axis=-1, keepdims=True))
           - jnp.exp(jnp.sum(lq2_ref[...] * lk2_ref[...], axis=-1, keepdims=True)) + lam_init)
    out_gain = sw_ref[...] * (1.0 - lam_init)

    for j in range(n_tiles):
        cols = slice(j * tile, (j + 1) * tile)
        qt_scr[j, :, :tile] = q1t_ref[0, :, cols]
        qt_scr[j, :, tile:] = q2t_ref[0, :, cols]
        vt_scr[j, 0:V_DIM, :] = vt_ref[0, :, cols]
        vt_scr[j, V_DIM:, :] = jnp.ones((V_ROWS - V_DIM, tile), BF16)
    p_scr[...] = jnp.zeros(p_scr.shape, BF16)
    acc_scr[...] = jnp.zeros(acc_scr.shape, F32)

    near_bias = jnp.maximum(jnp.abs(bias_ref[0, 1]),
                            jnp.where(bias_ref[0, 0] == -jnp.inf, 0.0, jnp.abs(bias_ref[0, 0])))
    score_bound = (HEAD_DIM * Q_GAIN_SCALE * ATT_BOUND_SLACK * jnp.max(jnp.abs(qw_ref[...]))
                   * jnp.max(jnp.abs(kw_ref[...])) + jnp.max(near_bias))
    no_max = score_bound <= ATT_NO_MAX_BOUND

    def advance(qa, ja):
        wrap = ja == qa
        at_end = jnp.logical_and(wrap, qa == n_tiles - 1)
        qa2 = jnp.where(jnp.logical_and(wrap, jnp.logical_not(at_end)), qa + 1, qa)
        ja2 = jnp.where(at_end, ja, jnp.where(wrap, 0, ja + 1))
        return qa2, ja2, at_end

    def scores(qa, ja):
        k = k_ref[0, pl.ds(pl.multiple_of(ja * tile, tile), tile), :]
        sa = _dot(k, qt_scr[qa])
        bt = bias_ref[0, jnp.minimum(qa - ja, 2)]
        return sa[:, :tile] + bt, sa[:, tile:] + bt

    def fast_step(slot, carry):
        qa, ja, ia, ib, jb, ic, jc = carry
        acc_scr[ic] = acc_scr[ic] + _dot(vt_scr[jc], p_scr[(slot + 1) % 3])
        s1, s2 = scores(qa, ja)
        p_scr[slot, :, :tile] = jnp.exp2(s1).astype(BF16)
        p_scr[slot, :, tile:] = jnp.exp2(s2).astype(BF16)
        qa2, ja2, at_end = advance(qa, ja)
        ia2 = jnp.where(jnp.logical_or(at_end, ia == n_tiles), n_tiles, qa2)
        return qa2, ja2, ia2, ia, ja, ib, jb

    def fast_steps(_, carry):
        for u in range(ATT_FAST_UNROLL):
            carry = fast_step(u % 3, carry)
        return carry

    @pl.when(no_max)
    def _():
        zero, spare = jnp.int32(0), jnp.int32(n_tiles)
        lax.fori_loop(0, pl.cdiv(n_pairs + 2, ATT_FAST_UNROLL), fast_steps,
                      (zero, zero, zero, spare, zero, spare, zero))

    def half_step(slot, carry):
        qa, ja, qb, jb, qc, jc = carry
        other = 1 - slot
        acc_scr[qc] = alpha_scr[slot] * acc_scr[qc] + _dot(vt_scr[jc], p_scr[slot])
        s = s_scr[other]
        m_old = jnp.where(jb == 0, -jnp.inf, m_scr[...])
        m_new = jnp.maximum(m_old, jnp.max(s, axis=0, keepdims=True))
        alpha_scr[other] = jnp.exp2(m_old - m_new)
        p_scr[other] = jnp.exp2(s - m_new).astype(BF16)
        m_scr[...] = m_new
        s_scr[slot, :, :tile], s_scr[slot, :, tile:] = scores(qa, ja)
        qa2, ja2, _ = advance(qa, ja)
        return qa2, ja2, qa, ja, qb, jb

    def steps(_, carry):
        for u in range(ATT_UNROLL):
            carry = half_step(u % 2, carry)
        return carry

    @pl.when(jnp.logical_not(no_max))
    def _():
        s_scr[...] = jnp.zeros(s_scr.shape, F32)
        alpha_scr[...] = jnp.zeros(alpha_scr.shape, F32)
        m_scr[...] = jnp.zeros(m_scr.shape, F32)
        zero, one = jnp.int32(0), jnp.int32(1)
        lax.fori_loop(0, (n_pairs + 2) // ATT_UNROLL, steps, (zero, zero, zero, one, zero, one))

    fin_unroll = math.gcd(n_tiles, ATT_FINAL_UNROLL)

    def finalize(i, c):
        for u in range(fin_unroll):
            qi = i * fin_unroll + u
            acc = acc_scr[qi]
            o = acc[:V_DIM, :] / acc[V_DIM:V_DIM + 1, :]
            ot = o[:, :tile] - lam * o[:, tile:]
            ot = ot * lax.rsqrt(jnp.mean(ot * ot, axis=0, keepdims=True) + EPS)
            o_ref[0, pl.ds(pl.multiple_of(qi * tile, tile), tile), :] = (ot.T * out_gain).astype(o_ref.dtype)
        return c

    lax.fori_loop(0, n_tiles // fin_unroll, finalize, 0)


def _attention(k3, qvt, bias_tiles, qw, kw, lq1, lk1, lq2, lk2, sw, lam_init):
    b, s, _ = k3.shape
    tile = ATT_TILE
    n_tiles = s // tile
    head_t = lambda off: pl.BlockSpec((1, V_DIM, s), lambda bi, hi: (bi, off + hi, 0))
    small = lambda a: pl.BlockSpec(a.shape, lambda bi, hi: (0,) * a.ndim)
    return pl.pallas_call(
        functools.partial(_attn_kernel, lam_init=lam_init),
        grid=(b, N_HEADS),
        in_specs=[pl.BlockSpec((1, s, V_DIM), lambda bi, hi: (bi, 0, hi)),
                  head_t(0), head_t(N_HEADS), head_t(2 * N_HEADS),
                  pl.BlockSpec((1, 3, tile, tile), lambda bi, hi: (hi, 0, 0, 0)),
                  small(qw), small(kw), small(lq1), small(lk1), small(lq2), small(lk2), small(sw)],
        out_specs=pl.BlockSpec((1, s, V_DIM), lambda bi, hi: (bi, 0, hi)),
        out_shape=jax.ShapeDtypeStruct((b, s, N_HEADS * V_DIM), BF16),
        scratch_shapes=[pltpu.VMEM((n_tiles, V_DIM, 2 * tile), BF16),
                        pltpu.VMEM((n_tiles, V_ROWS, tile), BF16),
                        pltpu.VMEM((2, tile, 2 * tile), F32),
                        pltpu.VMEM((3, tile, 2 * tile), BF16),
                        pltpu.VMEM((2, 1, 2 * tile), F32),
                        pltpu.VMEM((n_tiles + 1, V_ROWS, 2 * tile), F32),
                        pltpu.VMEM((1, 2 * tile), F32)],
        compiler_params=pltpu.CompilerParams(dimension_semantics=("arbitrary", "arbitrary"),
                                             vmem_limit_bytes=VMEM_LIMIT),
        name="diff_attention",
    )(k3, qvt, qvt, qvt, bias_tiles, qw, kw, lq1, lk1, lq2, lk2, sw)


def _conv_units(win_scr, y_scr, w_ref, b_ref, lnw_ref, lnb_ref, o_ref):
    ts, n_ch = y_scr.shape
    sub = 8
    base = CONV_HALO - (CONV_WIDTH - 1)

    def taps(r0, c, after=None):
        lanes = slice(c * LANES, (c + 1) * LANES)
        y = jnp.broadcast_to(b_ref[:, lanes], (CONV_CHUNK, LANES))
        if after is not None:
            y = y + jnp.tile(_zero_like_token(after), (CONV_CHUNK // TOKEN_ROWS, 1))
        for r in range(sub):
            part = None
            for o in range(r, base + CONV_WIDTH, sub):
                if o < base:
                    continue
                rows = slice(r0 + o - r, r0 + o - r + CONV_CHUNK + sub)
                term = win_scr[rows, lanes] * w_ref[o - base:o - base + 1, lanes]
                part = term if part is None else part + term
            y = y + part[r:r + CONV_CHUNK, :]
        y_scr[r0:r0 + CONV_CHUNK, lanes] = y
        return y[0:TOKEN_ROWS, :]

    def norm(r0, after=None):
        acc = y_scr[r0:r0 + CONV_LN_ROWS, :]
        mu = jnp.mean(acc, axis=-1, keepdims=True)
        xc = acc - mu
        var = jnp.mean(xc * xc, axis=-1, keepdims=True)
        y = xc * lax.rsqrt(var + EPS) * lnw_ref[...] + lnb_ref[...]
        y = y * _sigmoid(y)
        o_ref[r0:r0 + CONV_LN_ROWS, :] = y.astype(o_ref.dtype)
        return y[0:TOKEN_ROWS, 0:LANES]

    units = []
    for r0 in range(0, ts, CONV_CHUNK):
        units += [functools.partial(taps, r0, c) for c in range(n_ch // LANES)]
        units += [functools.partial(norm, r) for r in range(r0, r0 + CONV_CHUNK, CONV_LN_ROWS)]
    return units


def _post_kernel(glu_ref, a_ref, g_ref, x_ref, wpa_ref, wpc_ref, wo_ref, nw_ref, wgu_ref, wd_ref,
                 cw_ref, cb_ref, lnw_ref, lnb_ref, o_ref, win_scr, y_scr, halo_scr, c_scr, h_scr, act_scr,
                 *, tiles_per_seq):
    step = pl.program_id(0)
    d = x_ref.shape[1]
    ff = wd_ref.shape[0]
    tm = glu_ref.shape[0]

    @pl.when(step == 0)
    def _():
        c_scr[...] = jnp.zeros(c_scr.shape, c_scr.dtype)

    @pl.when(step % tiles_per_seq == 0)
    def _():
        halo_scr[...] = jnp.zeros(halo_scr.shape, F32)

    win_scr[0:CONV_HALO, :] = halo_scr[...]
    win_scr[CONV_HALO:CONV_HALO + tm, :] = glu_ref[...].astype(F32)
    win_scr[CONV_HALO + tm:, :] = jnp.zeros((win_scr.shape[0] - CONV_HALO - tm, win_scr.shape[1]), F32)
    halo_scr[...] = win_scr[tm:tm + CONV_HALO, :]

    ya = _dot(a_ref[...], wpa_ref[...])
    yb = _dot(c_scr[...], wpc_ref[...])
    h_scr[...] = (g_ref[:, :d].astype(F32) * ya + g_ref[:, d:].astype(F32) * yb).astype(BF16)
    residual = []

    def out_proj():
        x1 = x_ref[...] + _dot(h_scr[...], wo_ref[...])
        residual.append(x1)
        h_scr[...] = _rms_rows(x1, nw_ref[...]).astype(BF16)
        return x1[0:TOKEN_ROWS, 0:LANES]

    def ffn_chunk(c):
        h = h_scr[...]
        g = _dot(h, wgu_ref[:, c * FF_CHUNK:(c + 1) * FF_CHUNK])
        u = _dot(h, wgu_ref[:, ff + c * FF_CHUNK:ff + (c + 1) * FF_CHUNK])
        act = g * _sigmoid(g) * u
        act_scr[:, c * FF_CHUNK:(c + 1) * FF_CHUNK] = act.astype(BF16)
        return act[0:TOKEN_ROWS, 0:LANES]

    def down():
        o_ref[...] = residual[0] + _dot(act_scr[...], wd_ref[...])

    units = _conv_units(win_scr, y_scr, cw_ref, cb_ref, lnw_ref, lnb_ref, c_scr)
    stages = ([(out_proj, h_scr)] + [(functools.partial(ffn_chunk, c), h_scr) for c in range(ff // FF_CHUNK)]
              + [(down, act_scr)])
    twentieth = len(units) // 20
    share = [3 * twentieth, 2 * twentieth] + [twentieth] * (len(stages) - 2)
    done = 0
    corner = (slice(0, TOKEN_ROWS), slice(0, LANES))
    tokens = [None, None]
    for i, (stage, operand) in enumerate(stages):
        upto = done + share[i]
        zero = sum(_zero_like_token(unit(after=tokens[i])) for unit in units[done:upto])
        operand[corner] = operand[corner] + zero.astype(BF16)
        done = upto
        tokens.append(stage())
    for unit in units[done:]:
        unit(after=tokens[len(stages) - 1])


def _post(glu2, a2, g2, x2, seq, wpa, wpc, wo, nw, wgu, wd, cw, cb, lnw, lnb):
    t, d = x2.shape
    tm = min(POST_TILE, seq)
    n = t // tm
    cur = lambda width: pl.BlockSpec((tm, width), lambda i: (jnp.minimum(i, n - 1), 0))
    prev = lambda width: pl.BlockSpec((tm, width), lambda i: (jnp.maximum(i - 1, 0), 0))
    weights = (wpa, wpc, wo, nw, wgu, wd, cw, cb, lnw, lnb)
    return pl.pallas_call(
        functools.partial(_post_kernel, tiles_per_seq=seq // tm),
        grid=(n + 1,),
        in_specs=[cur(glu2.shape[1]), prev(a2.shape[1]), prev(g2.shape[1]), prev(d)]
                 + [_resident(w.shape) for w in weights],
        out_specs=prev(d),
        out_shape=jax.ShapeDtypeStruct((t, d), F32),
        scratch_shapes=[pltpu.VMEM((tm + CONV_HALO + 8, glu2.shape[1]), F32),
                        pltpu.VMEM((tm, glu2.shape[1]), F32),
                        pltpu.VMEM((CONV_HALO, glu2.shape[1]), F32),
                        pltpu.VMEM((tm, glu2.shape[1]), BF16),
                        pltpu.VMEM((tm, d), BF16),
                        pltpu.VMEM((tm, wd.shape[0]), BF16)],
        compiler_params=pltpu.CompilerParams(dimension_semantics=("arbitrary",),
                                             vmem_limit_bytes=VMEM_LIMIT),
        name="post_attention",
    )(glu2, a2, g2, x2, *weights)


def kernel(x, attn_norm_w, w_in, q_norm_w, k_norm_w, lam_q1, lam_k1, lam_q2, lam_k2, subln_w, w_proj_attn,
           conv_w, conv_b, conv_ln_w, conv_ln_b, w_proj_conv, gate_b, w_out, ffn_norm_w, w_gate_up, w_down,
           rel_bias):
    bsz, s, d = x.shape
    depth = w_in.shape[0]
    d_attn = N_HEADS * V_DIM
    assert s % ATT_TILE == 0 and s % ROW_TILE == 0 and s % POST_TILE == 0
    assert w_down.shape[1] % FF_CHUNK == 0 and d == d_attn

    row = lambda v: v.reshape(1, -1).astype(F32)
    two = lambda v: jnp.concatenate([v, v]).reshape(1, -1).astype(F32)
    bias_tiles = _bias_tiles(rel_bias.astype(F32))
    x2 = x.reshape(bsz * s, d).astype(F32)

    for l in range(depth):
        lam_init = 0.8 - 0.6 * math.exp(-0.3 * l)
        w = w_in[l].astype(BF16)
        wqkv, wua, wub, wg = (w[:, :3 * d_attn], w[:, 3 * d_attn:3 * d_attn + d],
                              w[:, 3 * d_attn + d:3 * d_attn + 2 * d], w[:, 3 * d_attn + 2 * d:])
        qw, kw = two(q_norm_w[l]), two(k_norm_w[l])
        k2, qvt, glu, gates = _inproj(x2, s, row(attn_norm_w[l]), wqkv, wua, wub, wg, row(gate_b[l]), qw, kw)
        a = _attention(k2.reshape(bsz, s, d_attn), qvt, bias_tiles, qw, kw,
                       row(lam_q1[l]), row(lam_k1[l]), row(lam_q2[l]), row(lam_k2[l]), row(subln_w[l]), lam_init)
        x2 = _post(glu, a.reshape(bsz * s, d_attn), gates, x2, s,
                   w_proj_attn[l].astype(BF16), w_proj_conv[l].astype(BF16), w_out[l].astype(BF16),
                   row(ffn_norm_w[l]), w_gate_up[l].astype(BF16), w_down[l].astype(BF16),
                   conv_w[l].astype(F32), row(conv_b[l]), row(conv_ln_w[l]), row(conv_ln_b[l]))
    return x2.reshape(bsz, s, d).astype(x.dtype)
```

```python
import functools
import math

import numpy as np
import jax
import jax.numpy as jnp
from jax import lax
from jax.experimental import pallas as pl
from jax.experimental.pallas import tpu as pltpu

N_HEADS = 8
HEAD_DIM = 64
V_DIM = 2 * HEAD_DIM
CONV_WIDTH = 31
NUM_BUCKETS = 32
MAX_DISTANCE = 128
EPS = 1e-6
LOG2E = math.log2(math.e)
Q_GAIN_SCALE = HEAD_DIM ** -0.5 * LOG2E

LANES = 128
ROW_TILE = 512
ATT_TILE = 256
ATT_UNROLL = 6
ATT_FAST_UNROLL = 24
ATT_FINAL_UNROLL = 4
ATT_NO_MAX_BOUND = 60.0
ATT_BOUND_SLACK = 1.02
CONV_CHUNK = 64
CONV_LN_ROWS = 32
CONV_HALO = 32
POST_TILE = 256
TOKEN_ROWS = 16
FF_CHUNK = 256
VMEM_LIMIT = 56 * 1024 * 1024

BF16 = jnp.bfloat16
F32 = jnp.float32


def _t5_bucket_starts():
    rel = np.arange(0, 4 * MAX_DISTANCE, dtype=np.int32)
    max_exact = NUM_BUCKETS // 2
    nf = np.maximum(rel, 1).astype(np.float32)
    large = max_exact + (np.log(nf / np.float32(max_exact)) / np.float32(math.log(MAX_DISTANCE / max_exact))
                         * np.float32(NUM_BUCKETS - max_exact)).astype(np.int32)
    large = np.minimum(large, NUM_BUCKETS - 1)
    bucket = np.where(rel < max_exact, rel, large)
    starts = [int(np.argmax(bucket == b)) for b in range(NUM_BUCKETS)]
    assert all(bucket[s] == b for b, s in enumerate(starts)) and starts == sorted(starts)
    return starts


_BUCKET_STARTS = _t5_bucket_starts()
assert _BUCKET_STARTS[-1] <= ATT_TILE // 2


def _resident(shape):
    return pl.BlockSpec(shape, lambda *_: (0,) * len(shape), pipeline_mode=pl.Buffered(1))


def _dot(a, b):
    return jnp.dot(a, b, preferred_element_type=F32)


def _sigmoid(x):
    return 1.0 / (1.0 + jnp.exp(-x))


def _zero_like_token(tok):
    bits = lax.bitcast_convert_type(tok, jnp.uint32)
    return lax.bitcast_convert_type((bits >> 16) >> 16, F32)


def _rms_rows(x, w):
    ms = jnp.mean(x * x, axis=-1, keepdims=True)
    return x * lax.rsqrt(ms + EPS) * w


def _head_rms(acc, w128):
    rows = acc.shape[0]
    lo = lax.broadcasted_iota(jnp.int32, (rows, LANES), 1) < HEAD_DIM
    outs = []
    for hh in range(acc.shape[1] // LANES):
        blk = acc[:, hh * LANES:(hh + 1) * LANES]
        sq = blk * blk
        s_lo = jnp.sum(jnp.where(lo, sq, 0.0), axis=-1, keepdims=True)
        s_hi = jnp.sum(jnp.where(lo, 0.0, sq), axis=-1, keepdims=True)
        r = jnp.where(lo, lax.rsqrt(s_lo / HEAD_DIM + EPS), lax.rsqrt(s_hi / HEAD_DIM + EPS))
        outs.append(blk * r * w128)
    return jnp.concatenate(outs, axis=1)


def _inproj_kernel(x_ref, nw_ref, wqkv_ref, wua_ref, wub_ref, wg_ref, gb_ref, qw_ref, kw_ref,
                   k_ref, qvt_ref, glu_ref, gate_ref, *, chunk):
    d_attn = N_HEADS * V_DIM
    tm = x_ref.shape[0]
    h = _rms_rows(x_ref[...], nw_ref[...]).astype(BF16)
    qw = qw_ref[...] * Q_GAIN_SCALE
    kw = kw_ref[...]
    first_half = lax.broadcasted_iota(jnp.int32, (tm, V_DIM), 1) < HEAD_DIM
    for c in range(3 * d_attn // chunk):
        cols = slice(c * chunk, (c + 1) * chunk)
        acc = _dot(h, wqkv_ref[:, cols])
        if d_attn <= c * chunk < 2 * d_attn:
            k_ref[:, c * chunk - d_attn:(c + 1) * chunk - d_attn] = _head_rms(acc, kw).astype(BF16)
            continue
        if c * chunk < d_attn:
            acc = _head_rms(acc, qw)
        for hh in range(chunk // V_DIM):
            blk = acc[:, hh * V_DIM:(hh + 1) * V_DIM]
            ch = c * chunk % d_attn + hh * V_DIM
            if c * chunk < d_attn:
                qvt_ref[0, ch:ch + V_DIM, :] = jnp.where(first_half, blk, 0.0).T.astype(BF16)
                qvt_ref[0, d_attn + ch:d_attn + ch + V_DIM, :] = jnp.where(first_half, 0.0, blk).T.astype(BF16)
            else:
                qvt_ref[0, 2 * d_attn + ch:2 * d_attn + ch + V_DIM, :] = blk.T.astype(BF16)
    for c in range(wua_ref.shape[1] // chunk):
        cols = slice(c * chunk, (c + 1) * chunk)
        ua = _dot(h, wua_ref[:, cols])
        ub = _dot(h, wub_ref[:, cols])
        glu_ref[:, cols] = (ua * _sigmoid(ub)).astype(BF16)
    for c in range(wg_ref.shape[1] // chunk):
        cols = slice(c * chunk, (c + 1) * chunk)
        g = _dot(h, wg_ref[:, cols]) + gb_ref[:, cols]
        gate_ref[:, cols] = _sigmoid(g).astype(BF16)


def _inproj(x2, seq, nw, wqkv, wua, wub, wg, gb, qw, kw):
    t, d = x2.shape
    tm = min(ROW_TILE, seq)
    per_seq = seq // tm
    d_attn = N_HEADS * V_DIM
    row = lambda n: pl.BlockSpec((tm, n), lambda i: (i, 0))
    return pl.pallas_call(
        functools.partial(_inproj_kernel, chunk=512),
        grid=(t // tm,),
        in_specs=[row(d)] + [_resident(a.shape) for a in (nw, wqkv, wua, wub, wg, gb, qw, kw)],
        out_specs=[row(d_attn),
                   pl.BlockSpec((1, 3 * d_attn, tm), lambda i: (i // per_seq, 0, i % per_seq)),
                   row(wua.shape[1]), row(wg.shape[1])],
        out_shape=[jax.ShapeDtypeStruct((t, d_attn), BF16),
                   jax.ShapeDtypeStruct((t // seq, 3 * d_attn, seq), BF16),
                   jax.ShapeDtypeStruct((t, wua.shape[1]), BF16),
                   jax.ShapeDtypeStruct((t, wg.shape[1]), BF16)],
        compiler_params=pltpu.CompilerParams(dimension_semantics=("arbitrary",),
                                             vmem_limit_bytes=VMEM_LIMIT),
        name="inproj",
    )(x2, nw, wqkv, wua, wub, wg, gb, qw, kw)


def _bias_tiles_kernel(tab_ref, out_ref):
    h = pl.program_id(0)
    tile = out_ref.shape[-1]
    key = lax.broadcasted_iota(jnp.int32, (tile, tile), 0)
    qry = lax.broadcasted_iota(jnp.int32, (tile, tile), 1)
    far = tab_ref[NUM_BUCKETS - 1, h]
    for d in range(3):
        rel = qry - key + d * tile
        bias = jnp.full((tile, tile), tab_ref[0, h], F32)
        for b in range(1, NUM_BUCKETS):
            bias = jnp.where(rel >= _BUCKET_STARTS[b], tab_ref[b, h], bias)
        bias = (bias - far) * LOG2E
        if d == 0:
            bias = jnp.where(rel >= 0, bias, -jnp.inf)
        out_ref[0, d] = bias


def _bias_tiles(rel_bias):
    return pl.pallas_call(
        _bias_tiles_kernel,
        grid=(N_HEADS,),
        in_specs=[pl.BlockSpec(memory_space=pltpu.SMEM)],
        out_specs=pl.BlockSpec((1, 3, ATT_TILE, ATT_TILE), lambda h: (h, 0, 0, 0)),
        out_shape=jax.ShapeDtypeStruct((N_HEADS, 3, ATT_TILE, ATT_TILE), F32),
        name="t5_bias_tiles",
    )(rel_bias)


V_ROWS = V_DIM + 16


def _attn_kernel(k_ref, q1t_ref, q2t_ref, vt_ref, bias_ref, qw_ref, kw_ref, lq1_ref, lk1_ref, lq2_ref, lk2_ref,
                 sw_ref, o_ref, qt_scr, vt_scr, s_scr, p_scr, alpha_scr, acc_scr, m_scr, *, lam_init):
    tile = ATT_TILE
    n_tiles = k_ref.shape[1] // tile
    n_pairs = n_tiles * (n_tiles + 1) // 2
    assert n_tiles >= 2 and ATT_UNROLL % 2 == 0 and (n_pairs + 2) % ATT_UNROLL == 0 and ATT_FAST_UNROLL % 3 == 0

    lam = (jnp.exp(jnp.sum(lq1_ref[...] * lk1_ref[...], axis=-1, keepdims=True))
           - jnp.exp(jnp.sum(lq2_ref[...] * lk2_ref[...], axis=-1, keepdims=True)) + lam_init)
    out_gain = sw_ref[...] * (1.0 - lam_init)

    for j in range(n_tiles):
        cols = slice(j * tile, (j + 1) * tile)
        qt_scr[j, :, :tile] = q1t_ref[0, :, cols]
        qt_scr[j, :, tile:] = q2t_ref[0, :, cols]
        vt_scr[j, 0:V_DIM, :] = vt_ref[0, :, cols]
        vt_scr[j, V_DIM:, :] = jnp.ones((V_ROWS - V_DIM, tile), BF16)
    p_scr[...] = jnp.zeros(p_scr.shape, BF16)
    acc_scr[...] = jnp.zeros(acc_scr.shape, F32)

    near_bias = jnp.maximum(jnp.abs(bias_ref[0, 1]),
                            jnp.where(bias_ref[0, 0] == -jnp.inf, 0.0, jnp.abs(bias_ref[0, 0])))
    score_bound = (HEAD_DIM * Q_GAIN_SCALE * ATT_BOUND_SLACK * jnp.max(jnp.abs(qw_ref[...]))
                   * jnp.max(jnp.abs(kw_ref[...])) + jnp.max(near_bias))
    no_max = score_bound <= ATT_NO_MAX_BOUND

    def advance(qa, ja):
        wrap = ja == qa
        at_end = jnp.logical_and(wrap, qa == n_tiles - 1)
        qa2 = jnp.where(jnp.logical_and(wrap, jnp.logical_not(at_end)), qa + 1, qa)
        ja2 = jnp.where(at_end, ja, jnp.where(wrap, 0, ja + 1))
        return qa2, ja2, at_end

    def scores(qa, ja):
        k = k_ref[0, pl.ds(pl.multiple_of(ja * tile, tile), tile), :]
        sa = _dot(k, qt_scr[qa])
        bt = bias_ref[0, jnp.minimum(qa - ja, 2)]
        return sa[:, :tile] + bt, sa[:, tile:] + bt

    def fast_step(slot, carry):
        qa, ja, ia, ib, jb, ic, jc = carry
        acc_scr[ic] = acc_scr[ic] + _dot(vt_scr[jc], p_scr[(slot + 1) % 3])
        s1, s2 = scores(qa, ja)
        p_scr[slot, :, :tile] = jnp.exp2(s1).astype(BF16)
        p_scr[slot, :, tile:] = jnp.exp2(s2).astype(BF16)
        qa2, ja2, at_end = advance(qa, ja)
        ia2 = jnp.where(jnp.logical_or(at_end, ia == n_tiles), n_tiles, qa2)
        return qa2, ja2, ia2, ia, ja, ib, jb

    def fast_steps(_, carry):
        for u in range(ATT_FAST_UNROLL):
            carry = fast_step(u % 3, carry)
        return carry

    @pl.when(no_max)
    def _():
        zero, spare = jnp.int32(0), jnp.int32(n_tiles)
        lax.fori_loop(0, pl.cdiv(n_pairs + 2, ATT_FAST_UNROLL), fast_steps,
                      (zero, zero, zero, spare, zero, spare, zero))

    def half_step(slot, carry):
        qa, ja, qb, jb, qc, jc = carry
        other = 1 - slot
        acc_scr[qc] = alpha_scr[slot] * acc_scr[qc] + _dot(vt_scr[jc], p_scr[slot])
        s = s_scr[other]
        m_old = jnp.where(jb == 0, -jnp.inf, m_scr[...])
        m_new = jnp.maximum(m_old, jnp.max(s, axis=0, keepdims=True))
        alpha_scr[other] = jnp.exp2(m_old - m_new)
        p_scr[other] = jnp.exp2(s - m_new).astype(BF16)
        m_scr[...] = m_new
        s_scr[slot, :, :tile], s_scr[slot, :, tile:] = scores(qa, ja)
        qa2, ja2, _ = advance(qa, ja)
        return qa2, ja2, qa, ja, qb, jb

    def steps(_, carry):
        for u in range(ATT_UNROLL):
            carry = half_step(u % 2, carry)
        return carry

    @pl.when(jnp.logical_not(no_max))
    def _():
        s_scr[...] = jnp.zeros(s_scr.shape, F32)
        alpha_scr[...] = jnp.zeros(alpha_scr.shape, F32)
        m_scr[...] = jnp.zeros(m_scr.shape, F32)
        zero, one = jnp.int32(0), jnp.int32(1)
        lax.fori_loop(0, (n_pairs + 2) // ATT_UNROLL, steps, (zero, zero, zero, one, zero, one))

    fin_unroll = math.gcd(n_tiles, ATT_FINAL_UNROLL)

    def finalize(i, c):
        for u in range(fin_unroll):
            qi = i * fin_unroll + u
            acc = acc_scr[qi]
            o = acc[:V_DIM, :] / acc[V_DIM:V_DIM + 1, :]
            ot = o[:, :tile] - lam * o[:, tile:]
            ot = ot * lax.rsqrt(jnp.mean(ot * ot, axis=0, keepdims=True) + EPS)
            o_ref[0, pl.ds(pl.multiple_of(qi * tile, tile), tile), :] = (ot.T * out_gain).astype(o_ref.dtype)
        return c

    lax.fori_loop(0, n_tiles // fin_unroll, finalize, 0)


def _attention(k3, qvt, bias_tiles, qw, kw, lq1, lk1, lq2, lk2, sw, lam_init):
    b, s, _ = k3.shape
    tile = ATT_TILE
    n_tiles = s // tile
    head_t = lambda off: pl.BlockSpec((1, V_DIM, s), lambda bi, hi: (bi, off + hi, 0))
    small = lambda a: pl.BlockSpec(a.shape, lambda bi, hi: (0,) * a.ndim)
    return pl.pallas_call(
        functools.partial(_attn_kernel, lam_init=lam_init),
        grid=(b, N_HEADS),
        in_specs=[pl.BlockSpec((1, s, V_DIM), lambda bi, hi: (bi, 0, hi)),
                  head_t(0), head_t(N_HEADS), head_t(2 * N_HEADS),
                  pl.BlockSpec((1, 3, tile, tile), lambda bi, hi: (hi, 0, 0, 0)),
                  small(qw), small(kw), small(lq1), small(lk1), small(lq2), small(lk2), small(sw)],
        out_specs=pl.BlockSpec((1, s, V_DIM), lambda bi, hi: (bi, 0, hi)),
        out_shape=jax.ShapeDtypeStruct((b, s, N_HEADS * V_DIM), BF16),
        scratch_shapes=[pltpu.VMEM((n_tiles, V_DIM, 2 * tile), BF16),
                        pltpu.VMEM((n_tiles, V_ROWS, tile), BF16),
                        pltpu.VMEM((2, tile, 2 * tile), F32),
                        pltpu.VMEM((3, tile, 2 * tile), BF16),
                        pltpu.VMEM((2, 1, 2 * tile), F32),
                        pltpu.VMEM((n_tiles + 1, V_ROWS, 2 * tile), F32),
                        pltpu.VMEM((1, 2 * tile), F32)],
        compiler_params=pltpu.CompilerParams(dimension_semantics=("arbitrary", "arbitrary"),
                                             vmem_limit_bytes=VMEM_LIMIT),
        name="diff_attention",
    )(k3, qvt, qvt, qvt, bias_tiles, qw, kw, lq1, lk1, lq2, lk2, sw)


def _conv_units(win_scr, y_scr, w_ref, b_ref, lnw_ref, lnb_ref, o_ref):
    ts, n_ch = y_scr.shape
    sub = 8
    base = CONV_HALO - (CONV_WIDTH - 1)

    def taps(r0, c, after=None):
        lanes = slice(c * LANES, (c + 1) * LANES)
        y = jnp.broadcast_to(b_ref[:, lanes], (CONV_CHUNK, LANES))
        if after is not None:
            y = y + jnp.tile(_zero_like_token(after), (CONV_CHUNK // TOKEN_ROWS, 1))
        for r in range(sub):
            part = None
            for o in range(r, base + CONV_WIDTH, sub):
                if o < base:
                    continue
                rows = slice(r0 + o - r, r0 + o - r + CONV_CHUNK + sub)
                term = win_scr[rows, lanes] * w_ref[o - base:o - base + 1, lanes]
                part = term if part is None else part + term
            y = y + part[r:r + CONV_CHUNK, :]
        y_scr[r0:r0 + CONV_CHUNK, lanes] = y
        return y[0:TOKEN_ROWS, :]

    def norm(r0, after=None):
        acc = y_scr[r0:r0 + CONV_LN_ROWS, :]
        mu = jnp.mean(acc, axis=-1, keepdims=True)
        xc = acc - mu
        var = jnp.mean(xc * xc, axis=-1, keepdims=True)
        y = xc * lax.rsqrt(var + EPS) * lnw_ref[...] + lnb_ref[...]
        y = y * _sigmoid(y)
        o_ref[r0:r0 + CONV_LN_ROWS, :] = y.astype(o_ref.dtype)
        return y[0:TOKEN_ROWS, 0:LANES]

    units = []
    for r0 in range(0, ts, CONV_CHUNK):
        units += [functools.partial(taps, r0, c) for c in range(n_ch // LANES)]
        units += [functools.partial(norm, r) for r in range(r0, r0 + CONV_CHUNK, CONV_LN_ROWS)]
    return units


def _post_kernel(glu_ref, a_ref, g_ref, x_ref, wpa_ref, wpc_ref, wo_ref, nw_ref, wgu_ref, wd_ref,
                 cw_ref, cb_ref, lnw_ref, lnb_ref, o_ref, win_scr, y_scr, halo_scr, c_scr, h_scr, act_scr,
                 *, tiles_per_seq):
    step = pl.program_id(0)
    d = x_ref.shape[1]
    ff = wd_ref.shape[0]
    tm = glu_ref.shape[0]

    @pl.when(step == 0)
    def _():
        c_scr[...] = jnp.zeros(c_scr.shape, c_scr.dtype)

    @pl.when(step % tiles_per_seq == 0)
    def _():
        halo_scr[...] = jnp.zeros(halo_scr.shape, F32)

    win_scr[0:CONV_HALO, :] = halo_scr[...]
    win_scr[CONV_HALO:CONV_HALO + tm, :] = glu_ref[...].astype(F32)
    win_scr[CONV_HALO + tm:, :] = jnp.zeros((win_scr.shape[0] - CONV_HALO - tm, win_scr.shape[1]), F32)
    halo_scr[...] = win_scr[tm:tm + CONV_HALO, :]

    corner = (slice(0, TOKEN_ROWS), slice(0, LANES))
    kept = {}

    def proj_attn(zero):
        kept["ya"] = _dot(a_ref[...], wpa_ref[...])
        return kept["ya"][corner]

    def proj_conv(zero):
        c_scr[corner] = c_scr[corner] + zero.astype(BF16)
        kept["yb"] = _dot(c_scr[...], wpc_ref[...])
        return kept["yb"][corner]

    def out_proj(zero):
        h_scr[...] = (g_ref[:, :d].astype(F32) * kept["ya"] + g_ref[:, d:].astype(F32) * kept["yb"]).astype(BF16)
        h_scr[corner] = h_scr[corner] + zero.astype(BF16)
        kept["x1"] = x_ref[...] + _dot(h_scr[...], wo_ref[...])
        h_scr[...] = _rms_rows(kept["x1"], nw_ref[...]).astype(BF16)
        return kept["x1"][corner]

    def ffn_chunk(c, zero):
        h_scr[corner] = h_scr[corner] + zero.astype(BF16)
        h = h_scr[...]
        g = _dot(h, wgu_ref[:, c * FF_CHUNK:(c + 1) * FF_CHUNK])
        u = _dot(h, wgu_ref[:, ff + c * FF_CHUNK:ff + (c + 1) * FF_CHUNK])
        act = g * _sigmoid(g) * u
        act_scr[:, c * FF_CHUNK:(c + 1) * FF_CHUNK] = act.astype(BF16)
        return act[corner]

    def down(zero):
        act_scr[corner] = act_scr[corner] + zero.astype(BF16)
        o_ref[...] = kept["x1"] + _dot(act_scr[...], wd_ref[...])

    units = _conv_units(win_scr, y_scr, cw_ref, cb_ref, lnw_ref, lnb_ref, c_scr)
    stages = [proj_attn, proj_conv, out_proj] + [functools.partial(ffn_chunk, c) for c in range(ff // FF_CHUNK)] + [down]
    twentieth = len(units) // 20
    share = [0, 2 * twentieth, 2 * twentieth] + [twentieth] * (len(stages) - 3)
    assert share[0] + share[1] <= glu_ref.shape[1] // LANES
    done = 0
    tokens = [None, None]
    for i, stage in enumerate(stages):
        upto = done + share[i]
        zero = sum((_zero_like_token(unit(after=tokens[i])) for unit in units[done:upto]),
                   jnp.zeros((TOKEN_ROWS, LANES), F32))
        done = upto
        tokens.append(stage(zero))
    for unit in units[done:]:
        unit(after=tokens[len(stages) - 1])


def _post(glu2, a2, g2, x2, seq, wpa, wpc, wo, nw, wgu, wd, cw, cb, lnw, lnb):
    t, d = x2.shape
    tm = min(POST_TILE, seq)
    n = t // tm
    cur = lambda width: pl.BlockSpec((tm, width), lambda i: (jnp.minimum(i, n - 1), 0))
    prev = lambda width: pl.BlockSpec((tm, width), lambda i: (jnp.maximum(i - 1, 0), 0))
    weights = (wpa, wpc, wo, nw, wgu, wd, cw, cb, lnw, lnb)
    return pl.pallas_call(
        functools.partial(_post_kernel, tiles_per_seq=seq // tm),
        grid=(n + 1,),
        in_specs=[cur(glu2.shape[1]), prev(a2.shape[1]), prev(g2.shape[1]), prev(d)]
                 + [_resident(w.shape) for w in weights],
        out_specs=prev(d),
        out_shape=jax.ShapeDtypeStruct((t, d), F32),
        scratch_shapes=[pltpu.VMEM((tm + CONV_HALO + 8, glu2.shape[1]), F32),
                        pltpu.VMEM((tm, glu2.shape[1]), F32),
                        pltpu.VMEM((CONV_HALO, glu2.shape[1]), F32),
                        pltpu.VMEM((tm, glu2.shape[1]), BF16),
                        pltpu.VMEM((tm, d), BF16),
                        pltpu.VMEM((tm, wd.shape[0]), BF16)],
        compiler_params=pltpu.CompilerParams(dimension_semantics=("arbitrary",),
                                             vmem_limit_bytes=VMEM_LIMIT),
        name="post_attention",
    )(glu2, a2, g2, x2, *weights)


def kernel(x, attn_norm_w, w_in, q_norm_w, k_norm_w, lam_q1, lam_k1, lam_q2, lam_k2, subln_w, w_proj_attn,
           conv_w, conv_b, conv_ln_w, conv_ln_b, w_proj_conv, gate_b, w_out, ffn_norm_w, w_gate_up, w_down,
           rel_bias):
    bsz, s, d = x.shape
    depth = w_in.shape[0]
    d_attn = N_HEADS * V_DIM
    assert s % ATT_TILE == 0 and s % ROW_TILE == 0 and s % POST_TILE == 0
    assert w_down.shape[1] % FF_CHUNK == 0 and d == d_attn

    row = lambda v: v.reshape(1, -1).astype(F32)
    two = lambda v: jnp.concatenate([v, v]).reshape(1, -1).astype(F32)
    bias_tiles = _bias_tiles(rel_bias.astype(F32))
    x2 = x.reshape(bsz * s, d).astype(F32)

    for l in range(depth):
        lam_init = 0.8 - 0.6 * math.exp(-0.3 * l)
        w = w_in[l].astype(BF16)
        wqkv, wua, wub, wg = (w[:, :3 * d_attn], w[:, 3 * d_attn:3 * d_attn + d],
                              w[:, 3 * d_attn + d:3 * d_attn + 2 * d], w[:, 3 * d_attn + 2 * d:])
        qw, kw = two(q_norm_w[l]), two(k_norm_w[l])
        k2, qvt, glu, gates = _inproj(x2, s, row(attn_norm_w[l]), wqkv, wua, wub, wg, row(gate_b[l]), qw, kw)
        a = _attention(k2.reshape(bsz, s, d_attn), qvt, bias_tiles, qw, kw,
                       row(lam_q1[l]), row(lam_k1[l]), row(lam_q2[l]), row(lam_k2[l]), row(subln_w[l]), lam_init)
        x2 = _post(glu, a.reshape(bsz * s, d_attn), gates, x2, s,
                   w_proj_attn[l].astype(BF16), w_proj_conv[l].astype(BF16), w_out[l].astype(BF16),
                   row(ffn_norm_w[l]), w_gate_up[l].astype(BF16), w_down[l].astype(BF16),
                   conv_w[l].astype(F32), row(conv_b[l]), row(conv_ln_w[l]), row(conv_ln_b[l]))
    return x2.reshape(bsz, s, d).astype(x.dtype)
```

```python
import functools
import math

import numpy as np
import jax
import jax.numpy as jnp
from jax import lax
from jax.experimental import pallas as pl
from jax.experimental.pallas import tpu as pltpu

N_HEADS = 8
HEAD_DIM = 64
V_DIM = 2 * HEAD_DIM
CONV_WIDTH = 31
NUM_BUCKETS = 32
MAX_DISTANCE = 128
EPS = 1e-6
LOG2E = math.log2(math.e)
Q_GAIN_SCALE = HEAD_DIM ** -0.5 * LOG2E

LANES = 128
ROW_TILE = 512
ATT_TILE = 256
ATT_UNROLL = 6
ATT_FAST_UNROLL = 24
ATT_FINAL_UNROLL = 4
ATT_NO_MAX_BOUND = 60.0
ATT_BOUND_SLACK = 1.02
CONV_CHUNK = 64
CONV_LN_ROWS = 32
CONV_HALO = 32
POST_TILE = 256
TOKEN_ROWS = 16
FF_CHUNK = 256
VMEM_LIMIT = 56 * 1024 * 1024

BF16 = jnp.bfloat16
F32 = jnp.float32


def _t5_bucket_starts():
    rel = np.arange(0, 4 * MAX_DISTANCE, dtype=np.int32)
    max_exact = NUM_BUCKETS // 2
    nf = np.maximum(rel, 1).astype(np.float32)
    large = max_exact + (np.log(nf / np.float32(max_exact)) / np.float32(math.log(MAX_DISTANCE / max_exact))
                         * np.float32(NUM_BUCKETS - max_exact)).astype(np.int32)
    large = np.minimum(large, NUM_BUCKETS - 1)
    bucket = np.where(rel < max_exact, rel, large)
    starts = [int(np.argmax(bucket == b)) for b in range(NUM_BUCKETS)]
    assert all(bucket[s] == b for b, s in enumerate(starts)) and starts == sorted(starts)
    return starts


_BUCKET_STARTS = _t5_bucket_starts()
assert _BUCKET_STARTS[-1] <= ATT_TILE // 2


def _resident(shape):
    return pl.BlockSpec(shape, lambda *_: (0,) * len(shape), pipeline_mode=pl.Buffered(1))


def _dot(a, b):
    return jnp.dot(a, b, preferred_element_type=F32)


def _sigmoid(x):
    return 1.0 / (1.0 + jnp.exp(-x))


def _zero_like_token(tok):
    bits = lax.bitcast_convert_type(tok, jnp.uint32)
    return lax.bitcast_convert_type((bits >> 16) >> 16, F32)


def _rms_rows(x, w):
    ms = jnp.mean(x * x, axis=-1, keepdims=True)
    return x * lax.rsqrt(ms + EPS) * w


def _head_rms(acc, w128):
    rows = acc.shape[0]
    lo = lax.broadcasted_iota(jnp.int32, (rows, LANES), 1) < HEAD_DIM
    outs = []
    for hh in range(acc.shape[1] // LANES):
        blk = acc[:, hh * LANES:(hh + 1) * LANES]
        sq = blk * blk
        s_lo = jnp.sum(jnp.where(lo, sq, 0.0), axis=-1, keepdims=True)
        s_hi = jnp.sum(jnp.where(lo, 0.0, sq), axis=-1, keepdims=True)
        r = jnp.where(lo, lax.rsqrt(s_lo / HEAD_DIM + EPS), lax.rsqrt(s_hi / HEAD_DIM + EPS))
        outs.append(blk * r * w128)
    return jnp.concatenate(outs, axis=1)


def _inproj_kernel(x_ref, nw_ref, wqkv_ref, wua_ref, wub_ref, wg_ref, gb_ref, qw_ref, kw_ref,
                   k_ref, qvt_ref, glu_ref, gate_ref, *, chunk):
    d_attn = N_HEADS * V_DIM
    tm = x_ref.shape[0]
    h = _rms_rows(x_ref[...], nw_ref[...]).astype(BF16)
    qw = qw_ref[...] * Q_GAIN_SCALE
    kw = kw_ref[...]
    first_half = lax.broadcasted_iota(jnp.int32, (tm, V_DIM), 1) < HEAD_DIM
    for c in range(3 * d_attn // chunk):
        cols = slice(c * chunk, (c + 1) * chunk)
        acc = _dot(h, wqkv_ref[:, cols])
        if d_attn <= c * chunk < 2 * d_attn:
            k_ref[:, c * chunk - d_attn:(c + 1) * chunk - d_attn] = _head_rms(acc, kw).astype(BF16)
            continue
        if c * chunk < d_attn:
            acc = _head_rms(acc, qw)
        for hh in range(chunk // V_DIM):
            blk = acc[:, hh * V_DIM:(hh + 1) * V_DIM]
            ch = c * chunk % d_attn + hh * V_DIM
            if c * chunk < d_attn:
                qvt_ref[0, ch:ch + V_DIM, :] = jnp.where(first_half, blk, 0.0).T.astype(BF16)
                qvt_ref[0, d_attn + ch:d_attn + ch + V_DIM, :] = jnp.where(first_half, 0.0, blk).T.astype(BF16)
            else:
                qvt_ref[0, 2 * d_attn + ch:2 * d_attn + ch + V_DIM, :] = blk.T.astype(BF16)
    for c in range(wua_ref.shape[1] // chunk):
        cols = slice(c * chunk, (c + 1) * chunk)
        ua = _dot(h, wua_ref[:, cols])
        ub = _dot(h, wub_ref[:, cols])
        glu_ref[:, cols] = (ua * _sigmoid(ub)).astype(BF16)
    for c in range(wg_ref.shape[1] // chunk):
        cols = slice(c * chunk, (c + 1) * chunk)
        g = _dot(h, wg_ref[:, cols]) + gb_ref[:, cols]
        gate_ref[:, cols] = _sigmoid(g).astype(BF16)


def _inproj(x2, seq, nw, wqkv, wua, wub, wg, gb, qw, kw):
    t, d = x2.shape
    tm = min(ROW_TILE, seq)
    per_seq = seq // tm
    d_attn = N_HEADS * V_DIM
    row = lambda n: pl.BlockSpec((tm, n), lambda i: (i, 0))
    return pl.pallas_call(
        functools.partial(_inproj_kernel, chunk=512),
        grid=(t // tm,),
        in_specs=[row(d)] + [_resident(a.shape) for a in (nw, wqkv, wua, wub, wg, gb, qw, kw)],
        out_specs=[row(d_attn),
                   pl.BlockSpec((1, 3 * d_attn, tm), lambda i: (i // per_seq, 0, i % per_seq)),
                   row(wua.shape[1]), row(wg.shape[1])],
        out_shape=[jax.ShapeDtypeStruct((t, d_attn), BF16),
                   jax.ShapeDtypeStruct((t // seq, 3 * d_attn, seq), BF16),
                   jax.ShapeDtypeStruct((t, wua.shape[1]), BF16),
                   jax.ShapeDtypeStruct((t, wg.shape[1]), BF16)],
        compiler_params=pltpu.CompilerParams(dimension_semantics=("arbitrary",),
                                             vmem_limit_bytes=VMEM_LIMIT),
        name="inproj",
    )(x2, nw, wqkv, wua, wub, wg, gb, qw, kw)


def _bias_tiles_kernel(tab_ref, out_ref):
    h = pl.program_id(0)
    tile = out_ref.shape[-1]
    key = lax.broadcasted_iota(jnp.int32, (tile, tile), 0)
    qry = lax.broadcasted_iota(jnp.int32, (tile, tile), 1)
    far = tab_ref[NUM_BUCKETS - 1, h]
    for d in range(3):
        rel = qry - key + d * tile
        bias = jnp.full((tile, tile), tab_ref[0, h], F32)
        for b in range(1, NUM_BUCKETS):
            bias = jnp.where(rel >= _BUCKET_STARTS[b], tab_ref[b, h], bias)
        bias = (bias - far) * LOG2E
        if d == 0:
            bias = jnp.where(rel >= 0, bias, -jnp.inf)
        out_ref[0, d] = bias


def _bias_tiles(rel_bias):
    return pl.pallas_call(
        _bias_tiles_kernel,
        grid=(N_HEADS,),
        in_specs=[pl.BlockSpec(memory_space=pltpu.SMEM)],
        out_specs=pl.BlockSpec((1, 3, ATT_TILE, ATT_TILE), lambda h: (h, 0, 0, 0)),
        out_shape=jax.ShapeDtypeStruct((N_HEADS, 3, ATT_TILE, ATT_TILE), F32),
        name="t5_bias_tiles",
    )(rel_bias)


V_ROWS = V_DIM + 16


def _attn_kernel(k_ref, q1t_ref, q2t_ref, vt_ref, bias_ref, qw_ref, kw_ref, lq1_ref, lk1_ref, lq2_ref, lk2_ref,
                 sw_ref, o_ref, qt_scr, vt_scr, s_scr, p_scr, alpha_scr, acc_scr, m_scr, *, lam_init):
    tile = ATT_TILE
    n_tiles = k_ref.shape[1] // tile
    n_pairs = n_tiles * (n_tiles + 1) // 2
    assert n_tiles >= 2 and ATT_UNROLL % 2 == 0 and (n_pairs + 2) % ATT_UNROLL == 0 and ATT_FAST_UNROLL % 3 == 0

    lam = (jnp.exp(jnp.sum(lq1_ref[...] * lk1_ref[...], axis=-1, keepdims=True))
           - jnp.exp(jnp.sum(lq2_ref[...] * lk2_ref[...], axis=-1, keepdims=True)) + lam_init)
    out_gain = sw_ref[...] * (1.0 - lam_init)

    for j in range(n_tiles):
        cols = slice(j * tile, (j + 1) * tile)
        qt_scr[j, :, :tile] = q1t_ref[0, :, cols]
        qt_scr[j, :, tile:] = q2t_ref[0, :, cols]
        vt_scr[j, 0:V_DIM, :] = vt_ref[0, :, cols]
        vt_scr[j, V_DIM:, :] = jnp.ones((V_ROWS - V_DIM, tile), BF16)
    p_scr[...] = jnp.zeros(p_scr.shape, BF16)
    acc_scr[...] = jnp.zeros(acc_scr.shape, F32)

    near_bias = jnp.maximum(jnp.abs(bias_ref[0, 1]),
                            jnp.where(bias_ref[0, 0] == -jnp.inf, 0.0, jnp.abs(bias_ref[0, 0])))
    score_bound = (HEAD_DIM * Q_GAIN_SCALE * ATT_BOUND_SLACK * jnp.max(jnp.abs(qw_ref[...]))
                   * jnp.max(jnp.abs(kw_ref[...])) + jnp.max(near_bias))
    no_max = score_bound <= ATT_NO_MAX_BOUND

    def advance(qa, ja):
        wrap = ja == qa
        at_end = jnp.logical_and(wrap, qa == n_tiles - 1)
        qa2 = jnp.where(jnp.logical_and(wrap, jnp.logical_not(at_end)), qa + 1, qa)
        ja2 = jnp.where(at_end, ja, jnp.where(wrap, 0, ja + 1))
        return qa2, ja2, at_end

    def scores(qa, ja):
        k = k_ref[0, pl.ds(pl.multiple_of(ja * tile, tile), tile), :]
        sa = _dot(k, qt_scr[qa])
        bt = bias_ref[0, jnp.minimum(qa - ja, 2)]
        return sa[:, :tile] + bt, sa[:, tile:] + bt

    def fast_step(slot, carry):
        qa, ja, ia, ib, jb, ic, jc = carry
        s1, s2 = scores(qa, ja)
        p_scr[slot, :, :tile] = jnp.exp2(s1).astype(BF16)
        p_scr[slot, :, tile:] = jnp.exp2(s2).astype(BF16)
        acc_scr[ic] = acc_scr[ic] + _dot(vt_scr[jc], p_scr[(slot + 1) % 3])
        qa2, ja2, at_end = advance(qa, ja)
        ia2 = jnp.where(jnp.logical_or(at_end, ia == n_tiles), n_tiles, qa2)
        return qa2, ja2, ia2, ia, ja, ib, jb

    def fast_steps(_, carry):
        for u in range(ATT_FAST_UNROLL):
            carry = fast_step(u % 3, carry)
        return carry

    @pl.when(no_max)
    def _():
        zero, spare = jnp.int32(0), jnp.int32(n_tiles)
        lax.fori_loop(0, pl.cdiv(n_pairs + 2, ATT_FAST_UNROLL), fast_steps,
                      (zero, zero, zero, spare, zero, spare, zero))

    def half_step(slot, carry):
        qa, ja, qb, jb, qc, jc = carry
        other = 1 - slot
        acc_scr[qc] = alpha_scr[slot] * acc_scr[qc] + _dot(vt_scr[jc], p_scr[slot])
        s = s_scr[other]
        m_old = jnp.where(jb == 0, -jnp.inf, m_scr[...])
        m_new = jnp.maximum(m_old, jnp.max(s, axis=0, keepdims=True))
        alpha_scr[other] = jnp.exp2(m_old - m_new)
        p_scr[other] = jnp.exp2(s - m_new).astype(BF16)
        m_scr[...] = m_new
        s_scr[slot, :, :tile], s_scr[slot, :, tile:] = scores(qa, ja)
        qa2, ja2, _ = advance(qa, ja)
        return qa2, ja2, qa, ja, qb, jb

    def steps(_, carry):
        for u in range(ATT_UNROLL):
            carry = half_step(u % 2, carry)
        return carry

    @pl.when(jnp.logical_not(no_max))
    def _():
        s_scr[...] = jnp.zeros(s_scr.shape, F32)
        alpha_scr[...] = jnp.zeros(alpha_scr.shape, F32)
        m_scr[...] = jnp.zeros(m_scr.shape, F32)
        zero, one = jnp.int32(0), jnp.int32(1)
        lax.fori_loop(0, (n_pairs + 2) // ATT_UNROLL, steps, (zero, zero, zero, one, zero, one))

    fin_unroll = math.gcd(n_tiles, ATT_FINAL_UNROLL)

    def finalize(i, c):
        for u in range(fin_unroll):
            qi = i * fin_unroll + u
            acc = acc_scr[qi]
            o = acc[:V_DIM, :] / acc[V_DIM:V_DIM + 1, :]
            ot = o[:, :tile] - lam * o[:, tile:]
            ot = ot * lax.rsqrt(jnp.mean(ot * ot, axis=0, keepdims=True) + EPS)
            o_ref[0, pl.ds(pl.multiple_of(qi * tile, tile), tile), :] = (ot.T * out_gain).astype(o_ref.dtype)
        return c

    lax.fori_loop(0, n_tiles // fin_unroll, finalize, 0)


def _attention(k3, qvt, bias_tiles, qw, kw, lq1, lk1, lq2, lk2, sw, lam_init):
    b, s, _ = k3.shape
    tile = ATT_TILE
    n_tiles = s // tile
    head_t = lambda off: pl.BlockSpec((1, V_DIM, s), lambda bi, hi: (bi, off + hi, 0))
    small = lambda a: pl.BlockSpec(a.shape, lambda bi, hi: (0,) * a.ndim)
    return pl.pallas_call(
        functools.partial(_attn_kernel, lam_init=lam_init),
        grid=(b, N_HEADS),
        in_specs=[pl.BlockSpec((1, s, V_DIM), lambda bi, hi: (bi, 0, hi)),
                  head_t(0), head_t(N_HEADS), head_t(2 * N_HEADS),
                  pl.BlockSpec((1, 3, tile, tile), lambda bi, hi: (hi, 0, 0, 0)),
                  small(qw), small(kw), small(lq1), small(lk1), small(lq2), small(lk2), small(sw)],
        out_specs=pl.BlockSpec((1, s, V_DIM), lambda bi, hi: (bi, 0, hi)),
        out_shape=jax.ShapeDtypeStruct((b, s, N_HEADS * V_DIM), BF16),
        scratch_shapes=[pltpu.VMEM((n_tiles, V_DIM, 2 * tile), BF16),
                        pltpu.VMEM((n_tiles, V_ROWS, tile), BF16),
                        pltpu.VMEM((2, tile, 2 * tile), F32),
                        pltpu.VMEM((3, tile, 2 * tile), BF16),
                        pltpu.VMEM((2, 1, 2 * tile), F32),
                        pltpu.VMEM((n_tiles + 1, V_ROWS, 2 * tile), F32),
                        pltpu.VMEM((1, 2 * tile), F32)],
        compiler_params=pltpu.CompilerParams(dimension_semantics=("arbitrary", "arbitrary"),
                                             vmem_limit_bytes=VMEM_LIMIT),
        name="diff_attention",
    )(k3, qvt, qvt, qvt, bias_tiles, qw, kw, lq1, lk1, lq2, lk2, sw)


def _conv_units(win_scr, y_scr, w_ref, b_ref, lnw_ref, lnb_ref, o_ref):
    ts, n_ch = y_scr.shape
    sub = 8
    base = CONV_HALO - (CONV_WIDTH - 1)

    def taps(r0, c, after=None):
        lanes = slice(c * LANES, (c + 1) * LANES)
        y = jnp.broadcast_to(b_ref[:, lanes], (CONV_CHUNK, LANES))
        if after is not None:
            y = y + jnp.tile(_zero_like_token(after), (CONV_CHUNK // TOKEN_ROWS, 1))
        for r in range(sub):
            part = None
            for o in range(r, base + CONV_WIDTH, sub):
                if o < base:
                    continue
                rows = slice(r0 + o - r, r0 + o - r + CONV_CHUNK + sub)
                term = win_scr[rows, lanes] * w_ref[o - base:o - base + 1, lanes]
                part = term if part is None else part + term
            y = y + part[r:r + CONV_CHUNK, :]
        y_scr[r0:r0 + CONV_CHUNK, lanes] = y
        return y[0:TOKEN_ROWS, :]

    def norm(r0, after=None):
        acc = y_scr[r0:r0 + CONV_LN_ROWS, :]
        mu = jnp.mean(acc, axis=-1, keepdims=True)
        xc = acc - mu
        var = jnp.mean(xc * xc, axis=-1, keepdims=True)
        y = xc * lax.rsqrt(var + EPS) * lnw_ref[...] + lnb_ref[...]
        y = y * _sigmoid(y)
        o_ref[r0:r0 + CONV_LN_ROWS, :] = y.astype(o_ref.dtype)
        return y[0:TOKEN_ROWS, 0:LANES]

    units = []
    for r0 in range(0, ts, CONV_CHUNK):
        units += [functools.partial(taps, r0, c) for c in range(n_ch // LANES)]
        units += [functools.partial(norm, r) for r in range(r0, r0 + CONV_CHUNK, CONV_LN_ROWS)]
    return units


def _post_kernel(glu_ref, a_ref, g_ref, x_ref, wpa_ref, wpc_ref, wo_ref, nw_ref, wgu_ref, wd_ref,
                 cw_ref, cb_ref, lnw_ref, lnb_ref, o_ref, win_scr, y_scr, halo_scr, c_scr, h_scr, act_scr,
                 *, tiles_per_seq):
    step = pl.program_id(0)
    d = x_ref.shape[1]
    ff = wd_ref.shape[0]
    tm = glu_ref.shape[0]

    @pl.when(step == 0)
    def _():
        c_scr[...] = jnp.zeros(c_scr.shape, c_scr.dtype)

    @pl.when(step % tiles_per_seq == 0)
    def _():
        halo_scr[...] = jnp.zeros(halo_scr.shape, F32)

    win_scr[0:CONV_HALO, :] = halo_scr[...]
    win_scr[CONV_HALO:CONV_HALO + tm, :] = glu_ref[...].astype(F32)
    win_scr[CONV_HALO + tm:, :] = jnp.zeros((win_scr.shape[0] - CONV_HALO - tm, win_scr.shape[1]), F32)
    halo_scr[...] = win_scr[tm:tm + CONV_HALO, :]

    corner = (slice(0, TOKEN_ROWS), slice(0, LANES))
    kept = {}

    def proj_attn(zero):
        kept["ya"] = _dot(a_ref[...], wpa_ref[...])
        return kept["ya"][corner]

    def proj_conv(zero):
        c_scr[corner] = c_scr[corner] + zero.astype(BF16)
        kept["yb"] = _dot(c_scr[...], wpc_ref[...])
        return kept["yb"][corner]

    def out_proj(zero):
        h_scr[...] = (g_ref[:, :d].astype(F32) * kept["ya"] + g_ref[:, d:].astype(F32) * kept["yb"]).astype(BF16)
        h_scr[corner] = h_scr[corner] + zero.astype(BF16)
        kept["x1"] = x_ref[...] + _dot(h_scr[...], wo_ref[...])
        h_scr[...] = _rms_rows(kept["x1"], nw_ref[...]).astype(BF16)
        return kept["x1"][corner]

    def ffn_chunk(c, zero):
        h_scr[corner] = h_scr[corner] + zero.astype(BF16)
        h = h_scr[...]
        g = _dot(h, wgu_ref[:, c * FF_CHUNK:(c + 1) * FF_CHUNK])
        u = _dot(h, wgu_ref[:, ff + c * FF_CHUNK:ff + (c + 1) * FF_CHUNK])
        act = g * _sigmoid(g) * u
        act_scr[:, c * FF_CHUNK:(c + 1) * FF_CHUNK] = act.astype(BF16)
        return act[corner]

    def down(zero):
        act_scr[corner] = act_scr[corner] + zero.astype(BF16)
        o_ref[...] = kept["x1"] + _dot(act_scr[...], wd_ref[...])

    units = _conv_units(win_scr, y_scr, cw_ref, cb_ref, lnw_ref, lnb_ref, c_scr)
    stages = [proj_attn, proj_conv, out_proj] + [functools.partial(ffn_chunk, c) for c in range(ff // FF_CHUNK)] + [down]
    twentieth = len(units) // 20
    share = [0, 2 * twentieth, 2 * twentieth] + [twentieth] * (len(stages) - 3)
    assert share[0] + share[1] <= glu_ref.shape[1] // LANES
    done = 0
    tokens = [None, None]
    for i, stage in enumerate(stages):
        upto = done + share[i]
        zero = sum((_zero_like_token(unit(after=tokens[i])) for unit in units[done:upto]),
                   jnp.zeros((TOKEN_ROWS, LANES), F32))
        done = upto
        tokens.append(stage(zero))
    for unit in units[done:]:
        unit(after=tokens[len(stages) - 1])


def _post(glu2, a2, g2, x2, seq, wpa, wpc, wo, nw, wgu, wd, cw, cb, lnw, lnb):
    t, d = x2.shape
    tm = min(POST_TILE, seq)
    n = t // tm
    cur = lambda width: pl.BlockSpec((tm, width), lambda i: (jnp.minimum(i, n - 1), 0))
    prev = lambda width: pl.BlockSpec((tm, width), lambda i: (jnp.maximum(i - 1, 0), 0))
    weights = (wpa, wpc, wo, nw, wgu, wd, cw, cb, lnw, lnb)
    return pl.pallas_call(
        functools.partial(_post_kernel, tiles_per_seq=seq // tm),
        grid=(n + 1,),
        in_specs=[cur(glu2.shape[1]), prev(a2.shape[1]), prev(g2.shape[1]), prev(d)]
                 + [_resident(w.shape) for w in weights],
        out_specs=prev(d),
        out_shape=jax.ShapeDtypeStruct((t, d), F32),
        scratch_shapes=[pltpu.VMEM((tm + CONV_HALO + 8, glu2.shape[1]), F32),
                        pltpu.VMEM((tm, glu2.shape[1]), F32),
                        pltpu.VMEM((CONV_HALO, glu2.shape[1]), F32),
                        pltpu.VMEM((tm, glu2.shape[1]), BF16),
                        pltpu.VMEM((tm, d), BF16),
                        pltpu.VMEM((tm, wd.shape[0]), BF16)],
        compiler_params=pltpu.CompilerParams(dimension_semantics=("arbitrary",),
                                             vmem_limit_bytes=VMEM_LIMIT),
        name="post_attention",
    )(glu2, a2, g2, x2, *weights)


def kernel(x, attn_norm_w, w_in, q_norm_w, k_norm_w, lam_q1, lam_k1, lam_q2, lam_k2, subln_w, w_proj_attn,
           conv_w, conv_b, conv_ln_w, conv_ln_b, w_proj_conv, gate_b, w_out, ffn_norm_w, w_gate_up, w_down,
           rel_bias):
    bsz, s, d = x.shape
    depth = w_in.shape[0]
    d_attn = N_HEADS * V_DIM
    assert s % ATT_TILE == 0 and s % ROW_TILE == 0 and s % POST_TILE == 0
    assert w_down.shape[1] % FF_CHUNK == 0 and d == d_attn

    row = lambda v: v.reshape(1, -1).astype(F32)
    two = lambda v: jnp.concatenate([v, v]).reshape(1, -1).astype(F32)
    bias_tiles = _bias_tiles(rel_bias.astype(F32))
    x2 = x.reshape(bsz * s, d).astype(F32)

    for l in range(depth):
        lam_init = 0.8 - 0.6 * math.exp(-0.3 * l)
        w = w_in[l].astype(BF16)
        wqkv, wua, wub, wg = (w[:, :3 * d_attn], w[:, 3 * d_attn:3 * d_attn + d],
                              w[:, 3 * d_attn + d:3 * d_attn + 2 * d], w[:, 3 * d_attn + 2 * d:])
        qw, kw = two(q_norm_w[l]), two(k_norm_w[l])
        k2, qvt, glu, gates = _inproj(x2, s, row(attn_norm_w[l]), wqkv, wua, wub, wg, row(gate_b[l]), qw, kw)
        a = _attention(k2.reshape(bsz, s, d_attn), qvt, bias_tiles, qw, kw,
                       row(lam_q1[l]), row(lam_k1[l]), row(lam_q2[l]), row(lam_k2[l]), row(subln_w[l]), lam_init)
        x2 = _post(glu, a.reshape(bsz * s, d_attn), gates, x2, s,
                   w_proj_attn[l].astype(BF16), w_proj_conv[l].astype(BF16), w_out[l].astype(BF16),
                   row(ffn_norm_w[l]), w_gate_up[l].astype(BF16), w_down[l].astype(BF16),
                   conv_w[l].astype(F32), row(conv_b[l]), row(conv_ln_w[l]), row(conv_ln_b[l]))
    return x2.reshape(bsz, s, d).astype(x.dtype)
```

```python
import functools
import math

import numpy as np
import jax
import jax.numpy as jnp
from jax import lax
from jax.experimental import pallas as pl
from jax.experimental.pallas import tpu as pltpu

N_HEADS = 8
HEAD_DIM = 64
V_DIM = 2 * HEAD_DIM
CONV_WIDTH = 31
NUM_BUCKETS = 32
MAX_DISTANCE = 128
EPS = 1e-6
LOG2E = math.log2(math.e)
Q_GAIN_SCALE = HEAD_DIM ** -0.5 * LOG2E

LANES = 128
ROW_TILE = 512
ATT_TILE = 256
ATT_UNROLL = 6
ATT_FAST_UNROLL = 48
ATT_FINAL_UNROLL = 4
ATT_NO_MAX_BOUND = 60.0
ATT_BOUND_SLACK = 1.02
CONV_CHUNK = 64
CONV_LN_ROWS = 32
CONV_HALO = 32
POST_TILE = 256
TOKEN_ROWS = 16
FF_CHUNK = 256
VMEM_LIMIT = 56 * 1024 * 1024

BF16 = jnp.bfloat16
F32 = jnp.float32


def _t5_bucket_starts():
    rel = np.arange(0, 4 * MAX_DISTANCE, dtype=np.int32)
    max_exact = NUM_BUCKETS // 2
    nf = np.maximum(rel, 1).astype(np.float32)
    large = max_exact + (np.log(nf / np.float32(max_exact)) / np.float32(math.log(MAX_DISTANCE / max_exact))
                         * np.float32(NUM_BUCKETS - max_exact)).astype(np.int32)
    large = np.minimum(large, NUM_BUCKETS - 1)
    bucket = np.where(rel < max_exact, rel, large)
    starts = [int(np.argmax(bucket == b)) for b in range(NUM_BUCKETS)]
    assert all(bucket[s] == b for b, s in enumerate(starts)) and starts == sorted(starts)
    return starts


_BUCKET_STARTS = _t5_bucket_starts()
assert _BUCKET_STARTS[-1] <= ATT_TILE // 2


def _resident(shape):
    return pl.BlockSpec(shape, lambda *_: (0,) * len(shape), pipeline_mode=pl.Buffered(1))


def _dot(a, b):
    return jnp.dot(a, b, preferred_element_type=F32)


def _sigmoid(x):
    return 1.0 / (1.0 + jnp.exp(-x))


def _zero_like_token(tok):
    bits = lax.bitcast_convert_type(tok, jnp.uint32)
    return lax.bitcast_convert_type((bits >> 16) >> 16, F32)


def _rms_rows(x, w):
    ms = jnp.mean(x * x, axis=-1, keepdims=True)
    return x * lax.rsqrt(ms + EPS) * w


def _head_rms(acc, w128):
    rows = acc.shape[0]
    lo = lax.broadcasted_iota(jnp.int32, (rows, LANES), 1) < HEAD_DIM
    outs = []
    for hh in range(acc.shape[1] // LANES):
        blk = acc[:, hh * LANES:(hh + 1) * LANES]
        sq = blk * blk
        s_lo = jnp.sum(jnp.where(lo, sq, 0.0), axis=-1, keepdims=True)
        s_hi = jnp.sum(jnp.where(lo, 0.0, sq), axis=-1, keepdims=True)
        r = jnp.where(lo, lax.rsqrt(s_lo / HEAD_DIM + EPS), lax.rsqrt(s_hi / HEAD_DIM + EPS))
        outs.append(blk * r * w128)
    return jnp.concatenate(outs, axis=1)


def _inproj_kernel(x_ref, nw_ref, wqkv_ref, wua_ref, wub_ref, wg_ref, gb_ref, qw_ref, kw_ref,
                   k_ref, qvt_ref, glu_ref, gate_ref, *, chunk):
    d_attn = N_HEADS * V_DIM
    tm = x_ref.shape[0]
    h = _rms_rows(x_ref[...], nw_ref[...]).astype(BF16)
    qw = qw_ref[...] * Q_GAIN_SCALE
    kw = kw_ref[...]
    first_half = lax.broadcasted_iota(jnp.int32, (tm, V_DIM), 1) < HEAD_DIM
    for c in range(3 * d_attn // chunk):
        cols = slice(c * chunk, (c + 1) * chunk)
        acc = _dot(h, wqkv_ref[:, cols])
        if d_attn <= c * chunk < 2 * d_attn:
            k_ref[:, c * chunk - d_attn:(c + 1) * chunk - d_attn] = _head_rms(acc, kw).astype(BF16)
            continue
        if c * chunk < d_attn:
            acc = _head_rms(acc, qw)
        for hh in range(chunk // V_DIM):
            blk = acc[:, hh * V_DIM:(hh + 1) * V_DIM]
            ch = c * chunk % d_attn + hh * V_DIM
            if c * chunk < d_attn:
                qvt_ref[0, ch:ch + V_DIM, :] = jnp.where(first_half, blk, 0.0).T.astype(BF16)
                qvt_ref[0, d_attn + ch:d_attn + ch + V_DIM, :] = jnp.where(first_half, 0.0, blk).T.astype(BF16)
            else:
                qvt_ref[0, 2 * d_attn + ch:2 * d_attn + ch + V_DIM, :] = blk.T.astype(BF16)
    for c in range(wua_ref.shape[1] // chunk):
        cols = slice(c * chunk, (c + 1) * chunk)
        ua = _dot(h, wua_ref[:, cols])
        ub = _dot(h, wub_ref[:, cols])
        glu_ref[:, cols] = (ua * _sigmoid(ub)).astype(BF16)
    for c in range(wg_ref.shape[1] // chunk):
        cols = slice(c * chunk, (c + 1) * chunk)
        g = _dot(h, wg_ref[:, cols]) + gb_ref[:, cols]
        gate_ref[:, cols] = _sigmoid(g).astype(BF16)


def _inproj(x2, seq, nw, wqkv, wua, wub, wg, gb, qw, kw):
    t, d = x2.shape
    tm = min(ROW_TILE, seq)
    per_seq = seq // tm
    d_attn = N_HEADS * V_DIM
    row = lambda n: pl.BlockSpec((tm, n), lambda i: (i, 0))
    return pl.pallas_call(
        functools.partial(_inproj_kernel, chunk=512),
        grid=(t // tm,),
        in_specs=[row(d)] + [_resident(a.shape) for a in (nw, wqkv, wua, wub, wg, gb, qw, kw)],
        out_specs=[row(d_attn),
                   pl.BlockSpec((1, 3 * d_attn, tm), lambda i: (i // per_seq, 0, i % per_seq)),
                   row(wua.shape[1]), row(wg.shape[1])],
        out_shape=[jax.ShapeDtypeStruct((t, d_attn), BF16),
                   jax.ShapeDtypeStruct((t // seq, 3 * d_attn, seq), BF16),
                   jax.ShapeDtypeStruct((t, wua.shape[1]), BF16),
                   jax.ShapeDtypeStruct((t, wg.shape[1]), BF16)],
        compiler_params=pltpu.CompilerParams(dimension_semantics=("arbitrary",),
                                             vmem_limit_bytes=VMEM_LIMIT),
        name="inproj",
    )(x2, nw, wqkv, wua, wub, wg, gb, qw, kw)


def _bias_tiles_kernel(tab_ref, out_ref):
    h = pl.program_id(0)
    tile = out_ref.shape[-1]
    key = lax.broadcasted_iota(jnp.int32, (tile, tile), 0)
    qry = lax.broadcasted_iota(jnp.int32, (tile, tile), 1)
    far = tab_ref[NUM_BUCKETS - 1, h]
    for d in range(3):
        rel = qry - key + d * tile
        bias = jnp.full((tile, tile), tab_ref[0, h], F32)
        for b in range(1, NUM_BUCKETS):
            bias = jnp.where(rel >= _BUCKET_STARTS[b], tab_ref[b, h], bias)
        bias = (bias - far) * LOG2E
        if d == 0:
            bias = jnp.where(rel >= 0, bias, -jnp.inf)
        out_ref[0, d] = bias


def _bias_tiles(rel_bias):
    return pl.pallas_call(
        _bias_tiles_kernel,
        grid=(N_HEADS,),
        in_specs=[pl.BlockSpec(memory_space=pltpu.SMEM)],
        out_specs=pl.BlockSpec((1, 3, ATT_TILE, ATT_TILE), lambda h: (h, 0, 0, 0)),
        out_shape=jax.ShapeDtypeStruct((N_HEADS, 3, ATT_TILE, ATT_TILE), F32),
        name="t5_bias_tiles",
    )(rel_bias)


V_ROWS = V_DIM + 16


def _attn_kernel(k_ref, q1t_ref, q2t_ref, vt_ref, bias_ref, qw_ref, kw_ref, lq1_ref, lk1_ref, lq2_ref, lk2_ref,
                 sw_ref, o_ref, qt_scr, vt_scr, s_scr, p_scr, alpha_scr, acc_scr, m_scr, *, lam_init):
    tile = ATT_TILE
    n_tiles = k_ref.shape[1] // tile
    n_pairs = n_tiles * (n_tiles + 1) // 2
    assert n_tiles >= 2 and ATT_UNROLL % 2 == 0 and (n_pairs + 2) % ATT_UNROLL == 0 and ATT_FAST_UNROLL % 3 == 0

    lam = (jnp.exp(jnp.sum(lq1_ref[...] * lk1_ref[...], axis=-1, keepdims=True))
           - jnp.exp(jnp.sum(lq2_ref[...] * lk2_ref[...], axis=-1, keepdims=True)) + lam_init)
    out_gain = sw_ref[...] * (1.0 - lam_init)

    for j in range(n_tiles):
        cols = slice(j * tile, (j + 1) * tile)
        qt_scr[j, :, :tile] = q1t_ref[0, :, cols]
        qt_scr[j, :, tile:] = q2t_ref[0, :, cols]
        vt_scr[j, 0:V_DIM, :] = vt_ref[0, :, cols]
        vt_scr[j, V_DIM:, :] = jnp.ones((V_ROWS - V_DIM, tile), BF16)
    p_scr[...] = jnp.zeros(p_scr.shape, BF16)
    acc_scr[...] = jnp.zeros(acc_scr.shape, F32)

    near_bias = jnp.maximum(jnp.abs(bias_ref[0, 1]),
                            jnp.where(bias_ref[0, 0] == -jnp.inf, 0.0, jnp.abs(bias_ref[0, 0])))
    score_bound = (HEAD_DIM * Q_GAIN_SCALE * ATT_BOUND_SLACK * jnp.max(jnp.abs(qw_ref[...]))
                   * jnp.max(jnp.abs(kw_ref[...])) + jnp.max(near_bias))
    no_max = score_bound <= ATT_NO_MAX_BOUND

    def advance(qa, ja):
        wrap = ja == qa
        at_end = jnp.logical_and(wrap, qa == n_tiles - 1)
        qa2 = jnp.where(jnp.logical_and(wrap, jnp.logical_not(at_end)), qa + 1, qa)
        ja2 = jnp.where(at_end, ja, jnp.where(wrap, 0, ja + 1))
        return qa2, ja2, at_end

    def scores(qa, ja):
        k = k_ref[0, pl.ds(pl.multiple_of(ja * tile, tile), tile), :]
        sa = _dot(k, qt_scr[qa])
        bt = bias_ref[0, jnp.minimum(qa - ja, 2)]
        return sa[:, :tile] + bt, sa[:, tile:] + bt

    def fast_step(slot, carry):
        qa, ja, ia, ib, jb, ic, jc = carry
        s1, s2 = scores(qa, ja)
        p_scr[slot, :, :tile] = jnp.exp2(s1).astype(BF16)
        p_scr[slot, :, tile:] = jnp.exp2(s2).astype(BF16)
        acc_scr[ic] = acc_scr[ic] + _dot(vt_scr[jc], p_scr[(slot + 1) % 3])
        qa2, ja2, at_end = advance(qa, ja)
        ia2 = jnp.where(jnp.logical_or(at_end, ia == n_tiles), n_tiles, qa2)
        return qa2, ja2, ia2, ia, ja, ib, jb

    def fast_steps(_, carry):
        for u in range(ATT_FAST_UNROLL):
            carry = fast_step(u % 3, carry)
        return carry

    @pl.when(no_max)
    def _():
        zero, spare = jnp.int32(0), jnp.int32(n_tiles)
        lax.fori_loop(0, pl.cdiv(n_pairs + 2, ATT_FAST_UNROLL), fast_steps,
                      (zero, zero, zero, spare, zero, spare, zero))

    def half_step(slot, carry):
        qa, ja, qb, jb, qc, jc = carry
        other = 1 - slot
        acc_scr[qc] = alpha_scr[slot] * acc_scr[qc] + _dot(vt_scr[jc], p_scr[slot])
        s = s_scr[other]
        m_old = jnp.where(jb == 0, -jnp.inf, m_scr[...])
        m_new = jnp.maximum(m_old, jnp.max(s, axis=0, keepdims=True))
        alpha_scr[other] = jnp.exp2(m_old - m_new)
        p_scr[other] = jnp.exp2(s - m_new).astype(BF16)
        m_scr[...] = m_new
        s_scr[slot, :, :tile], s_scr[slot, :, tile:] = scores(qa, ja)
        qa2, ja2, _ = advance(qa, ja)
        return qa2, ja2, qa, ja, qb, jb

    def steps(_, carry):
        for u in range(ATT_UNROLL):
            carry = half_step(u % 2, carry)
        return carry

    @pl.when(jnp.logical_not(no_max))
    def _():
        s_scr[...] = jnp.zeros(s_scr.shape, F32)
        alpha_scr[...] = jnp.zeros(alpha_scr.shape, F32)
        m_scr[...] = jnp.zeros(m_scr.shape, F32)
        zero, one = jnp.int32(0), jnp.int32(1)
        lax.fori_loop(0, (n_pairs + 2) // ATT_UNROLL, steps, (zero, zero, zero, one, zero, one))

    fin_unroll = math.gcd(n_tiles, ATT_FINAL_UNROLL)

    def finalize(i, c):
        for u in range(fin_unroll):
            qi = i * fin_unroll + u
            acc = acc_scr[qi]
            o = acc[:V_DIM, :] / acc[V_DIM:V_DIM + 1, :]
            ot = o[:, :tile] - lam * o[:, tile:]
            ot = ot * lax.rsqrt(jnp.mean(ot * ot, axis=0, keepdims=True) + EPS)
            o_ref[0, pl.ds(pl.multiple_of(qi * tile, tile), tile), :] = (ot.T * out_gain).astype(o_ref.dtype)
        return c

    lax.fori_loop(0, n_tiles // fin_unroll, finalize, 0)


def _attention(k3, qvt, bias_tiles, qw, kw, lq1, lk1, lq2, lk2, sw, lam_init):
    b, s, _ = k3.shape
    tile = ATT_TILE
    n_tiles = s // tile
    head_t = lambda off: pl.BlockSpec((1, V_DIM, s), lambda bi, hi: (bi, off + hi, 0))
    small = lambda a: pl.BlockSpec(a.shape, lambda bi, hi: (0,) * a.ndim)
    return pl.pallas_call(
        functools.partial(_attn_kernel, lam_init=lam_init),
        grid=(b, N_HEADS),
        in_specs=[pl.BlockSpec((1, s, V_DIM), lambda bi, hi: (bi, 0, hi)),
                  head_t(0), head_t(N_HEADS), head_t(2 * N_HEADS),
                  pl.BlockSpec((1, 3, tile, tile), lambda bi, hi: (hi, 0, 0, 0)),
                  small(qw), small(kw), small(lq1), small(lk1), small(lq2), small(lk2), small(sw)],
        out_specs=pl.BlockSpec((1, s, V_DIM), lambda bi, hi: (bi, 0, hi)),
        out_shape=jax.ShapeDtypeStruct((b, s, N_HEADS * V_DIM), BF16),
        scratch_shapes=[pltpu.VMEM((n_tiles, V_DIM, 2 * tile), BF16),
                        pltpu.VMEM((n_tiles, V_ROWS, tile), BF16),
                        pltpu.VMEM((2, tile, 2 * tile), F32),
                        pltpu.VMEM((3, tile, 2 * tile), BF16),
                        pltpu.VMEM((2, 1, 2 * tile), F32),
                        pltpu.VMEM((n_tiles + 1, V_ROWS, 2 * tile), F32),
                        pltpu.VMEM((1, 2 * tile), F32)],
        compiler_params=pltpu.CompilerParams(dimension_semantics=("arbitrary", "arbitrary"),
                                             vmem_limit_bytes=VMEM_LIMIT),
        name="diff_attention",
    )(k3, qvt, qvt, qvt, bias_tiles, qw, kw, lq1, lk1, lq2, lk2, sw)


def _conv_units(win_scr, y_scr, w_ref, b_ref, lnw_ref, lnb_ref, o_ref):
    ts, n_ch = y_scr.shape
    sub = 8
    base = CONV_HALO - (CONV_WIDTH - 1)

    def taps(r0, c, after=None):
        lanes = slice(c * LANES, (c + 1) * LANES)
        y = jnp.broadcast_to(b_ref[:, lanes], (CONV_CHUNK, LANES))
        if after is not None:
            y = y + jnp.tile(_zero_like_token(after), (CONV_CHUNK // TOKEN_ROWS, 1))
        for r in range(sub):
            part = None
            for o in range(r, base + CONV_WIDTH, sub):
                if o < base:
                    continue
                rows = slice(r0 + o - r, r0 + o - r + CONV_CHUNK + sub)
                term = win_scr[rows, lanes] * w_ref[o - base:o - base + 1, lanes]
                part = term if part is None else part + term
            y = y + part[r:r + CONV_CHUNK, :]
        y_scr[r0:r0 + CONV_CHUNK, lanes] = y
        return y[0:TOKEN_ROWS, :]

    def norm(r0, after=None):
        acc = y_scr[r0:r0 + CONV_LN_ROWS, :]
        mu = jnp.mean(acc, axis=-1, keepdims=True)
        xc = acc - mu
        var = jnp.mean(xc * xc, axis=-1, keepdims=True)
        y = xc * lax.rsqrt(var + EPS) * lnw_ref[...] + lnb_ref[...]
        y = y * _sigmoid(y)
        o_ref[r0:r0 + CONV_LN_ROWS, :] = y.astype(o_ref.dtype)
        return y[0:TOKEN_ROWS, 0:LANES]

    units = []
    for r0 in range(0, ts, CONV_CHUNK):
        units += [functools.partial(taps, r0, c) for c in range(n_ch // LANES)]
        units += [functools.partial(norm, r) for r in range(r0, r0 + CONV_CHUNK, CONV_LN_ROWS)]
    return units


def _post_kernel(glu_ref, a_ref, g_ref, x_ref, wpa_ref, wpc_ref, wo_ref, nw_ref, wgu_ref, wd_ref,
                 cw_ref, cb_ref, lnw_ref, lnb_ref, o_ref, win_scr, y_scr, halo_scr, c_scr, h_scr, act_scr,
                 *, tiles_per_seq):
    step = pl.program_id(0)
    d = x_ref.shape[1]
    ff = wd_ref.shape[0]
    tm = glu_ref.shape[0]

    @pl.when(step == 0)
    def _():
        c_scr[...] = jnp.zeros(c_scr.shape, c_scr.dtype)

    @pl.when(step % tiles_per_seq == 0)
    def _():
        halo_scr[...] = jnp.zeros(halo_scr.shape, F32)

    win_scr[0:CONV_HALO, :] = halo_scr[...]
    win_scr[CONV_HALO:CONV_HALO + tm, :] = glu_ref[...].astype(F32)
    win_scr[CONV_HALO + tm:, :] = jnp.zeros((win_scr.shape[0] - CONV_HALO - tm, win_scr.shape[1]), F32)
    halo_scr[...] = win_scr[tm:tm + CONV_HALO, :]

    corner = (slice(0, TOKEN_ROWS), slice(0, LANES))
    kept = {}

    def proj_attn(zero):
        kept["ya"] = _dot(a_ref[...], wpa_ref[...])
        return kept["ya"][corner]

    def proj_conv(zero):
        c_scr[corner] = c_scr[corner] + zero.astype(BF16)
        kept["yb"] = _dot(c_scr[...], wpc_ref[...])
        return kept["yb"][corner]

    def out_proj(zero):
        h_scr[...] = (g_ref[:, :d].astype(F32) * kept["ya"] + g_ref[:, d:].astype(F32) * kept["yb"]).astype(BF16)
        h_scr[corner] = h_scr[corner] + zero.astype(BF16)
        kept["x1"] = x_ref[...] + _dot(h_scr[...], wo_ref[...])
        h_scr[...] = _rms_rows(kept["x1"], nw_ref[...]).astype(BF16)
        return kept["x1"][corner]

    def ffn_chunk(c, zero):
        h_scr[corner] = h_scr[corner] + zero.astype(BF16)
        h = h_scr[...]
        g = _dot(h, wgu_ref[:, c * FF_CHUNK:(c + 1) * FF_CHUNK])
        u = _dot(h, wgu_ref[:, ff + c * FF_CHUNK:ff + (c + 1) * FF_CHUNK])
        act = g * _sigmoid(g) * u
        act_scr[:, c * FF_CHUNK:(c + 1) * FF_CHUNK] = act.astype(BF16)
        return act[corner]

    def down(zero):
        act_scr[corner] = act_scr[corner] + zero.astype(BF16)
        o_ref[...] = kept["x1"] + _dot(act_scr[...], wd_ref[...])

    units = _conv_units(win_scr, y_scr, cw_ref, cb_ref, lnw_ref, lnb_ref, c_scr)
    stages = [proj_attn, proj_conv, out_proj] + [functools.partial(ffn_chunk, c) for c in range(ff // FF_CHUNK)] + [down]
    twentieth = len(units) // 20
    share = [0, 2 * twentieth, 2 * twentieth] + [twentieth] * (len(stages) - 3)
    assert share[0] + share[1] <= glu_ref.shape[1] // LANES
    done = 0
    tokens = [None, None]
    for i, stage in enumerate(stages):
        upto = done + share[i]
        zero = sum((_zero_like_token(unit(after=tokens[i])) for unit in units[done:upto]),
                   jnp.zeros((TOKEN_ROWS, LANES), F32))
        done = upto
        tokens.append(stage(zero))
    for unit in units[done:]:
        unit(after=tokens[len(stages) - 1])


def _post(glu2, a2, g2, x2, seq, wpa, wpc, wo, nw, wgu, wd, cw, cb, lnw, lnb):
    t, d = x2.shape
    tm = min(POST_TILE, seq)
    n = t // tm
    cur = lambda width: pl.BlockSpec((tm, width), lambda i: (jnp.minimum(i, n - 1), 0))
    prev = lambda width: pl.BlockSpec((tm, width), lambda i: (jnp.maximum(i - 1, 0), 0))
    weights = (wpa, wpc, wo, nw, wgu, wd, cw, cb, lnw, lnb)
    return pl.pallas_call(
        functools.partial(_post_kernel, tiles_per_seq=seq // tm),
        grid=(n + 1,),
        in_specs=[cur(glu2.shape[1]), prev(a2.shape[1]), prev(g2.shape[1]), prev(d)]
                 + [_resident(w.shape) for w in weights],
        out_specs=prev(d),
        out_shape=jax.ShapeDtypeStruct((t, d), F32),
        scratch_shapes=[pltpu.VMEM((tm + CONV_HALO + 8, glu2.shape[1]), F32),
                        pltpu.VMEM((tm, glu2.shape[1]), F32),
                        pltpu.VMEM((CONV_HALO, glu2.shape[1]), F32),
                        pltpu.VMEM((tm, glu2.shape[1]), BF16),
                        pltpu.VMEM((tm, d), BF16),
                        pltpu.VMEM((tm, wd.shape[0]), BF16)],
        compiler_params=pltpu.CompilerParams(dimension_semantics=("arbitrary",),
                                             vmem_limit_bytes=VMEM_LIMIT),
        name="post_attention",
    )(glu2, a2, g2, x2, *weights)


def kernel(x, attn_norm_w, w_in, q_norm_w, k_norm_w, lam_q1, lam_k1, lam_q2, lam_k2, subln_w, w_proj_attn,
           conv_w, conv_b, conv_ln_w, conv_ln_b, w_proj_conv, gate_b, w_out, ffn_norm_w, w_gate_up, w_down,
           rel_bias):
    bsz, s, d = x.shape
    depth = w_in.shape[0]
    d_attn = N_HEADS * V_DIM
    assert s % ATT_TILE == 0 and s % ROW_TILE == 0 and s % POST_TILE == 0
    assert w_down.shape[1] % FF_CHUNK == 0 and d == d_attn

    row = lambda v: v.reshape(1, -1).astype(F32)
    two = lambda v: jnp.concatenate([v, v]).reshape(1, -1).astype(F32)
    bias_tiles = _bias_tiles(rel_bias.astype(F32))
    x2 = x.reshape(bsz * s, d).astype(F32)

    for l in range(depth):
        lam_init = 0.8 - 0.6 * math.exp(-0.3 * l)
        w = w_in[l].astype(BF16)
        wqkv, wua, wub, wg = (w[:, :3 * d_attn], w[:, 3 * d_attn:3 * d_attn + d],
                              w[:, 3 * d_attn + d:3 * d_attn + 2 * d], w[:, 3 * d_attn + 2 * d:])
        qw, kw = two(q_norm_w[l]), two(k_norm_w[l])
        k2, qvt, glu, gates = _inproj(x2, s, row(attn_norm_w[l]), wqkv, wua, wub, wg, row(gate_b[l]), qw, kw)
        a = _attention(k2.reshape(bsz, s, d_attn), qvt, bias_tiles, qw, kw,
                       row(lam_q1[l]), row(lam_k1[l]), row(lam_q2[l]), row(lam_k2[l]), row(subln_w[l]), lam_init)
        x2 = _post(glu, a.reshape(bsz * s, d_attn), gates, x2, s,
                   w_proj_attn[l].astype(BF16), w_proj_conv[l].astype(BF16), w_out[l].astype(BF16),
                   row(ffn_norm_w[l]), w_gate_up[l].astype(BF16), w_down[l].astype(BF16),
                   conv_w[l].astype(F32), row(conv_b[l]), row(conv_ln_w[l]), row(conv_ln_b[l]))
    return x2.reshape(bsz, s, d).astype(x.dtype)
```
